```python
import math
import jax, jax.numpy as jnp
from jax import lax
import numpy as np

D_MODEL = 1024
BATCH = 8
SEQ = 4096
DEPTH = 1

GRID_W = 64
CTX_LEN = 256
N_ADA = 6
HY_WIDTH = 512
HY_ORDER = 2
HY_EMB = 33
HY_BANDS = (HY_EMB - 1) // 2
HY_FFN = 64
HY_DECAY_TARGET = 1e-2
HY_FAST_PCT = 0.3
HY_SLOW_PCT = 1.5
AT_HEADS = 4
AT_D = 64
AT_W = AT_HEADS * 2 * AT_D
ROPE_AXIS = AT_D // 2
ROPE_BASE = 10000.0
Q_BLOCK = 128
D_FF = 2816
CONV_W = 3
LN_EPS = 1e-5
RMS_EPS = 1e-5
DN_ALPHA = (2.0 * DEPTH) ** 0.25
DN_BETA = (8.0 * DEPTH) ** -0.25
IN_COLS = 3 * HY_WIDTH + 3 * AT_W + 2 * D_MODEL

kernel_name = 'hybrid_hyena_diffattn_convffn_dit'


def layer_norm(x, g, b):
    xf = x.astype(jnp.float32)
    mu = jnp.mean(xf, -1, keepdims=True)
    var = jnp.mean(jnp.square(xf - mu), -1, keepdims=True)
    return ((xf - mu) * lax.rsqrt(var + LN_EPS)).astype(x.dtype) * g + b


def modulate(x, shift, scale):
    return x * (1.0 + scale) + shift


def dwconv3(u, w, b):
    up = jnp.pad(u, ((0, 0), (1, 1), (0, 0)))
    return up[:, :-2] * w[0] + up[:, 1:-1] * w[1] + up[:, 2:] * w[2] + b


def axial_rope_tables(L):
    rows = L // GRID_W
    row = jnp.repeat(jnp.arange(rows, dtype=jnp.float32), GRID_W)
    col = jnp.tile(jnp.arange(GRID_W, dtype=jnp.float32), rows)
    inv = ROPE_BASE ** (-jnp.arange(0, ROPE_AXIS, 2, dtype=jnp.float32) / ROPE_AXIS)
    ang_r = row[:, None] * inv[None]
    ang_c = col[:, None] * inv[None]
    return jnp.cos(ang_r), jnp.sin(ang_r), jnp.cos(ang_c), jnp.sin(ang_c)


def rotate_half(x, cos, sin):
    cos = cos[None, :, None, None, :].astype(x.dtype)
    sin = sin[None, :, None, None, :].astype(x.dtype)
    half = x.shape[-1] // 2
    x1, x2 = x[..., :half], x[..., half:]
    return jnp.concatenate([x1 * cos - x2 * sin, x2 * cos + x1 * sin], -1)


def axial_rope(x, cos_r, sin_r, cos_c, sin_c):
    return jnp.concatenate([rotate_half(x[..., :ROPE_AXIS], cos_r, sin_r),
                            rotate_half(x[..., ROPE_AXIS:], cos_c, sin_c)], -1)


def split_proj(z):
    B, L, _ = z.shape
    o1 = 3 * HY_WIDTH
    o2 = o1 + AT_W
    o3 = o2 + AT_W
    o4 = o3 + AT_W
    o5 = o4 + D_MODEL
    hy = z[..., :o1]
    q = z[..., o1:o2].reshape(B, L, AT_HEADS, 2, AT_D)
    k = z[..., o2:o3].reshape(B, L, AT_HEADS, 2, AT_D)
    v = z[..., o3:o4].reshape(B, L, AT_HEADS, 2 * AT_D)
    g_hy = z[..., o4:o5]
    g_at = z[..., o5:]
    return hy, q, k, v, g_hy, g_at


def hyena_filters(L, w1, b1, w2, b2, w3, b3, freq, w_out):
    t = jnp.linspace(0.0, 1.0, L, dtype=jnp.float32)[:, None]
    w = (2.0 * math.pi / L) * jnp.arange(L, dtype=jnp.float32)[:, None]
    f = jnp.linspace(1e-4, HY_BANDS - 1, HY_BANDS, dtype=jnp.float32)[None, :]
    z = jnp.concatenate([t, jnp.cos(f * w), -jnp.sin(f * w)], -1).astype(w1.dtype)
    hdn = jnp.sin(freq[0] * (z @ w1 + b1))
    hdn = jnp.sin(freq[1] * (hdn @ w2 + b2))
    hdn = jnp.sin(freq[2] * (hdn @ w3 + b3))
    h = (hdn @ w_out).reshape(L, HY_ORDER, 2, HY_WIDTH)
    min_decay = math.log(HY_DECAY_TARGET) / HY_SLOW_PCT
    max_decay = math.log(HY_DECAY_TARGET) / HY_FAST_PCT
    deltas = jnp.linspace(min_decay, max_decay, HY_WIDTH, dtype=jnp.float32)
    decay = jnp.exp(-t * jnp.abs(deltas)[None, :])
    h = h * decay[:, None, None, :].astype(h.dtype)
    fwd, bwd = h[:, :, 0], h[:, :, 1]
    return jnp.concatenate([fwd, jnp.zeros_like(fwd[:1]), bwd[:0:-1]], 0)


def long_conv(u, k, bias):
    L = u.shape[1]
    uf = jnp.fft.rfft(u.astype(jnp.float32), n=2 * L, axis=1)
    kf = jnp.fft.rfft(k.astype(jnp.float32), n=2 * L, axis=0)
    y = jnp.fft.irfft(uf * kf[None], n=2 * L, axis=1)[:, :L]
    return y.astype(u.dtype) + u * bias


def hyena_branch(z_hy, conv_w, conv_b, filt, bias):
    u = dwconv3(z_hy, conv_w, conv_b)
    v, x1, x2 = jnp.split(u, 3, -1)
    zz = x1 * long_conv(v, filt[:, 0], bias[0])
    return x2 * long_conv(zz, filt[:, 1], bias[1])


def diff_attn(q, k, v, lam):
    s = jnp.einsum('bqhcd,bkhcd->bhcqk', q, k).astype(jnp.float32) * (AT_D ** -0.5)
    p = jax.nn.softmax(s, axis=-1)
    a = p[:, :, 0] - lam * p[:, :, 1]
    return jnp.einsum('bhqk,bkhv->bqhv', a.astype(v.dtype), v)


def blocked_diff_attn(q, k, v, lam):
    B, L, H, C, d = q.shape
    nb = L // Q_BLOCK
    qb = jnp.moveaxis(q.reshape(B, nb, Q_BLOCK, H, C, d), 1, 0)
    ob = lax.map(lambda qi: diff_attn(qi, k, v, lam), qb)
    return jnp.moveaxis(ob, 0, 1).reshape(B, L, H, 2 * AT_D)


def merge_branches(y_hy, o_at, g_hy, g_at, subln_g, lam_init, w_hy_o, w_at_o, w_out):
    B, L = o_at.shape[:2]
    of = o_at.astype(jnp.float32)
    of = of * lax.rsqrt(jnp.mean(jnp.square(of), -1, keepdims=True) + RMS_EPS)
    o = (of.astype(o_at.dtype) * subln_g * (1.0 - lam_init)).reshape(B, L, AT_W)
    m = jax.nn.sigmoid(g_hy) * (y_hy @ w_hy_o) + jax.nn.sigmoid(g_at) * (o @ w_at_o)
    return m @ w_out


def conv_ffn(h, w_up, conv_w, conv_b, w_down):
    u = dwconv3(h @ w_up, conv_w, conv_b)
    a, g = jnp.split(u, 2, -1)
    return (jax.nn.silu(g) * a) @ w_down


def setup_inputs(seed: int = 0) -> dict:
    key = jax.random.key(seed)
    ks = jax.random.split(key, 40)
    D = D_MODEL
    L = DEPTH

    def nrm(i, shape, s):
        return s * jax.random.normal(ks[i], shape, jnp.float32)

    return {
        'x': nrm(0, (BATCH, SEQ, D), 1.0),
        'c': nrm(1, (BATCH, D), 1.0),
        'ctx': nrm(2, (BATCH, CTX_LEN, D), 1.0),
        'c_ctx': nrm(3, (D,), 1.0),
        'ln_in_g': 1.0 + nrm(4, (D,), 0.02),
        'ln_in_b': nrm(5, (D,), 0.02),
        'w_ada': nrm(6, (L, D, N_ADA * D), D ** -0.5),
        'b_ada': nrm(7, (L, N_ADA * D), 0.02),
        'w_in': nrm(8, (L, D, IN_COLS), D ** -0.5),
        'hy_conv_w': nrm(9, (L, CONV_W, 3 * HY_WIDTH), CONV_W ** -0.5),
        'hy_conv_b': nrm(10, (L, 3 * HY_WIDTH), 0.02),
        'hy_f_w1': nrm(11, (L, HY_EMB, HY_FFN), HY_EMB ** -0.5),
        'hy_f_b1': nrm(12, (L, HY_FFN), 0.02),
        'hy_f_w2': nrm(13, (L, HY_FFN, HY_FFN), HY_FFN ** -0.5),
        'hy_f_b2': nrm(14, (L, HY_FFN), 0.02),
        'hy_f_w3': nrm(15, (L, HY_FFN, HY_FFN), HY_FFN ** -0.5),
        'hy_f_b3': nrm(16, (L, HY_FFN), 0.02),
        'hy_f_freq': 1.0 + nrm(17, (L, 3, HY_FFN), 0.02),
        'hy_f_wout': nrm(18, (L, HY_FFN, HY_ORDER * 2 * HY_WIDTH), 0.1 * HY_FFN ** -0.5),
        'hy_bias': nrm(19, (L, HY_ORDER, HY_WIDTH), 1.0),
        'lam_q1': nrm(20, (L, AT_D), 0.1),
        'lam_k1': nrm(21, (L, AT_D), 0.1),
        'lam_q2': nrm(22, (L, AT_D), 0.1),
        'lam_k2': nrm(23, (L, AT_D), 0.1),
        'at_subln_g': 1.0 + nrm(24, (L, 2 * AT_D), 0.02),
        'w_hy_o': nrm(25, (L, HY_WIDTH, D), HY_WIDTH ** -0.5),
        'w_at_o': nrm(26, (L, AT_W, D), AT_W ** -0.5),
        'w_out': nrm(27, (L, D, D), DN_BETA * D ** -0.5),
        'ln1_g': 1.0 + nrm(28, (L, D), 0.02),
        'ln1_b': nrm(29, (L, D), 0.02),
        'ffn_w_up': nrm(30, (L, D, 2 * D_FF), D ** -0.5),
        'ffn_conv_w': nrm(31, (L, CONV_W, 2 * D_FF), CONV_W ** -0.5),
        'ffn_conv_b': nrm(32, (L, 2 * D_FF), 0.02),
        'ffn_w_down': nrm(33, (L, D_FF, D), DN_BETA * D_FF ** -0.5),
        'ln2_g': 1.0 + nrm(34, (L, D), 0.02),
        'ln2_b': nrm(35, (L, D), 0.02),
    }


def reference(x, c, ctx, c_ctx, ln_in_g, ln_in_b, w_ada, b_ada, w_in, hy_conv_w, hy_conv_b,
              hy_f_w1, hy_f_b1, hy_f_w2, hy_f_b2, hy_f_w3, hy_f_b3, hy_f_freq, hy_f_wout, hy_bias,
              lam_q1, lam_k1, lam_q2, lam_k2, at_subln_g, w_hy_o, w_at_o, w_out, ln1_g, ln1_b,
              ffn_w_up, ffn_conv_w, ffn_conv_b, ffn_w_down, ln2_g, ln2_b):
    seq_len = x.shape[1]
    ctx_len = ctx.shape[1]
    cos_r, sin_r, cos_c, sin_c = axial_rope_tables(seq_len)
    x = layer_norm(x, ln_in_g, ln_in_b)
    ctx_s = layer_norm(ctx, ln_in_g, ln_in_b)

    for l in range(DEPTH):
        last = l == DEPTH - 1
        lam_init = 0.8 - 0.6 * math.exp(-0.3 * l)
        mod_lat = jax.nn.silu(c) @ w_ada[l] + b_ada[l]
        mod_ctx = jax.nn.silu(c_ctx) @ w_ada[l] + b_ada[l]
        sh1, sc1, g1, sh2, sc2, g2 = jnp.split(mod_lat[:, None, :], N_ADA, -1)
        sh1c, sc1c, g1c, sh2c, sc2c, g2c = jnp.split(mod_ctx[None, None, :], N_ADA, -1)
        lam = (jnp.exp(jnp.sum(lam_q1[l] * lam_k1[l]).astype(jnp.float32))
               - jnp.exp(jnp.sum(lam_q2[l] * lam_k2[l]).astype(jnp.float32)) + lam_init)

        z_lat = modulate(x, sh1, sc1) @ w_in[l]
        z_ctx = modulate(ctx_s, sh1c, sc1c) @ w_in[l]
        hy_l, q_l, k_l, v_l, gh_l, ga_l = split_proj(z_lat)
        hy_c, q_c, k_c, v_c, gh_c, ga_c = split_proj(z_ctx)
        q_l = axial_rope(q_l, cos_r, sin_r, cos_c, sin_c)
        k_l = axial_rope(k_l, cos_r, sin_r, cos_c, sin_c)
        k_all = jnp.concatenate([k_c, k_l], 1)
        v_all = jnp.concatenate([v_c, v_l], 1)
        o_l = blocked_diff_attn(q_l, k_all, v_all, lam)
        filt_lat = hyena_filters(seq_len, hy_f_w1[l], hy_f_b1[l], hy_f_w2[l], hy_f_b2[l],
                                 hy_f_w3[l], hy_f_b3[l], hy_f_freq[l], hy_f_wout[l])
        y_hy_l = hyena_branch(hy_l, hy_conv_w[l], hy_conv_b[l], filt_lat, hy_bias[l])
        y_l = merge_branches(y_hy_l, o_l, gh_l, ga_l, at_subln_g[l], lam_init,
                             w_hy_o[l], w_at_o[l], w_out[l])

        if not last:
            o_c = diff_attn(q_c, k_c, v_c, lam)
            filt_ctx = hyena_filters(ctx_len, hy_f_w1[l], hy_f_b1[l], hy_f_w2[l], hy_f_b2[l],
                                     hy_f_w3[l], hy_f_b3[l], hy_f_freq[l], hy_f_wout[l])
            y_hy_c = hyena_branch(hy_c, hy_conv_w[l], hy_conv_b[l], filt_ctx, hy_bias[l])
            y_c = merge_branches(y_hy_c, o_c, gh_c, ga_c, at_subln_g[l], lam_init,
                                 w_hy_o[l], w_at_o[l], w_out[l])
            ctx_s = layer_norm(DN_ALPHA * ctx_s + g1c * y_c, ln1_g[l], ln1_b[l])
            f_c = conv_ffn(modulate(ctx_s, sh2c, sc2c), ffn_w_up[l], ffn_conv_w[l],
                           ffn_conv_b[l], ffn_w_down[l])
            ctx_s = layer_norm(DN_ALPHA * ctx_s + g2c * f_c, ln2_g[l], ln2_b[l])

        x = layer_norm(DN_ALPHA * x + g1 * y_l, ln1_g[l], ln1_b[l])
        f_l = conv_ffn(modulate(x, sh2, sc2), ffn_w_up[l], ffn_conv_w[l],
                       ffn_conv_b[l], ffn_w_down[l])
        x = layer_norm(DN_ALPHA * x + g2 * f_l, ln2_g[l], ln2_b[l])
    return x
```

```python
import functools
import math

import jax
import jax.numpy as jnp
from jax import lax
from jax.experimental import pallas as pl
from jax.experimental.pallas import tpu as pltpu

F32 = jnp.float32
BF16 = jnp.bfloat16

LN_EPS = 1e-5
RMS_EPS = 1e-5
GRID_W = 64
ROPE_BASE = 10000.0
N_ADA = 6
HY_DECAY_TARGET = 1e-2
HY_FAST_PCT = 0.3
HY_SLOW_PCT = 1.5
CONV_W = 3

V7X_LANES = 128
V7X_SUBLANES = 8
V7X_VMEM_BYTES = 64 * 1024 * 1024
VMEM_LIMIT = 56 * 1024 * 1024

HIGHEST = lax.Precision.HIGHEST


def _cparams(n_axes, vmem=None):
    return pltpu.CompilerParams(dimension_semantics=("arbitrary",) * n_axes, vmem_limit_bytes=vmem)


def _layer_norm(x, g, b):
    mu = jnp.mean(x, axis=-1, keepdims=True)
    xc = x - mu
    var = jnp.mean(xc * xc, axis=-1, keepdims=True)
    return xc * lax.rsqrt(var + LN_EPS) * g + b


def _sigmoid(x):
    return 1.0 / (1.0 + jnp.exp(-x))


def _dot(a, b):
    return jnp.dot(a, b, preferred_element_type=F32)


def _ada_kernel(c_ref, w_ref, b_ref, o_ref):
    c = c_ref[...]
    s = c * _sigmoid(c)
    o_ref[...] = jnp.dot(s, w_ref[...], preferred_element_type=F32, precision=HIGHEST) + b_ref[...]


def _ada(cc, w, b, tn=1536):
    rows, d = cc.shape
    n = w.shape[1]
    return pl.pallas_call(
        _ada_kernel,
        grid=(n // tn,),
        in_specs=[pl.BlockSpec((rows, d), lambda j: (0, 0)),
                  pl.BlockSpec((d, tn), lambda j: (0, j)),
                  pl.BlockSpec((1, tn), lambda j: (0, j))],
        out_specs=pl.BlockSpec((rows, tn), lambda j: (0, j)),
        out_shape=jax.ShapeDtypeStruct((rows, n), F32),
        compiler_params=_cparams(1, VMEM_LIMIT),
        name="ada",
    )(cc, w, b)


def _rope(x, cos, sin_a, sin_b, n_heads):
    outs = []
    for h in range(n_heads):
        xh = x[:, h * V7X_LANES:(h + 1) * V7X_LANES]
        up = pltpu.roll(xh, V7X_LANES - 16, 1)
        dn = pltpu.roll(xh, 16, 1)
        outs.append(xh * cos + up * sin_a + dn * sin_b)
    return jnp.concatenate(outs, axis=1)


def _inproj_kernel(x_ref, sh_ref, sc_ref, lng_ref, lnb_ref, w_ref, cos_ref, sa_ref, sb_ref,
                   hy_ref, q_ref, k_ref, v_ref, g_ref, *, hy_cols, at_w, q_scale):
    xn = _layer_norm(x_ref[0], lng_ref[...], lnb_ref[...])
    h = (xn * (1.0 + sc_ref[0]) + sh_ref[0]).astype(BF16)
    n_heads = at_w // V7X_LANES
    o1 = hy_cols
    o2, o3, o4 = o1 + at_w, o1 + 2 * at_w, o1 + 3 * at_w
    hy_ref[0] = _dot(h, w_ref[:, 0:o1]).astype(hy_ref.dtype)
    cos, sa, sb = cos_ref[...], sa_ref[...], sb_ref[...]
    q = _rope(_dot(h, w_ref[:, o1:o2]), cos, sa, sb, n_heads)
    q_ref[0] = (q * q_scale).astype(q_ref.dtype)
    k = _rope(_dot(h, w_ref[:, o2:o3]), cos, sa, sb, n_heads)
    k_ref[0] = k.astype(k_ref.dtype)
    v_ref[0] = _dot(h, w_ref[:, o3:o4]).astype(v_ref.dtype)
    g_ref[0] = _dot(h, w_ref[:, o4:]).astype(g_ref.dtype)


def _inproj(x, sh, sc, lng, lnb, w, cos, sa, sb, hy_cols, at_w, at_d, tm=512):
    B, S, D = x.shape
    ncols = w.shape[1]
    g_cols = ncols - hy_cols - 3 * at_w
    kern = functools.partial(_inproj_kernel, hy_cols=hy_cols, at_w=at_w, q_scale=at_d ** -0.5)
    row = lambda i, b: (b, i, 0)
    mod = lambda i, b: (b, 0, 0)
    const = lambda i, b: (0, 0)
    tab = lambda i, b: (i, 0)
    return pl.pallas_call(
        kern,
        grid=(S // tm, B),
        in_specs=[pl.BlockSpec((1, tm, D), row),
                  pl.BlockSpec((1, 1, D), mod), pl.BlockSpec((1, 1, D), mod),
                  pl.BlockSpec((1, D), const), pl.BlockSpec((1, D), const),
                  pl.BlockSpec((D, ncols), const, pipeline_mode=pl.Buffered(1)),
                  pl.BlockSpec((tm, V7X_LANES), tab), pl.BlockSpec((tm, V7X_LANES), tab),
                  pl.BlockSpec((tm, V7X_LANES), tab)],
        out_specs=[pl.BlockSpec((1, tm, hy_cols), row), pl.BlockSpec((1, tm, at_w), row),
                   pl.BlockSpec((1, tm, at_w), row), pl.BlockSpec((1, tm, at_w), row),
                   pl.BlockSpec((1, tm, g_cols), row)],
        out_shape=[jax.ShapeDtypeStruct((B, S, hy_cols), BF16), jax.ShapeDtypeStruct((B, S, at_w), BF16),
                   jax.ShapeDtypeStruct((B, S, at_w), BF16), jax.ShapeDtypeStruct((B, S, at_w), BF16),
                   jax.ShapeDtypeStruct((B, S, g_cols), BF16)],
        compiler_params=_cparams(2, VMEM_LIMIT),
        name="inproj",
    )(x, sh, sc, lng, lnb, w, cos, sa, sb)


def _ctx_kv_kernel(x_ref, sh_ref, sc_ref, lng_ref, lnb_ref, w_ref, k_ref, v_ref, *, at_w):
    xn = _layer_norm(x_ref[0], lng_ref[...], lnb_ref[...])
    h = (xn * (1.0 + sc_ref[...]) + sh_ref[...]).astype(BF16)
    k_ref[0] = _dot(h, w_ref[:, 0:at_w]).astype(k_ref.dtype)
    v_ref[0] = _dot(h, w_ref[:, at_w:]).astype(v_ref.dtype)


def _ctx_kv(ctx, sh, sc, lng, lnb, w_kv, at_w):
    B, C, D = ctx.shape
    const = lambda b: (0, 0)
    row = lambda b: (b, 0, 0)
    return pl.pallas_call(
        functools.partial(_ctx_kv_kernel, at_w=at_w),
        grid=(B,),
        in_specs=[pl.BlockSpec((1, C, D), row),
                  pl.BlockSpec((1, D), const), pl.BlockSpec((1, D), const),
                  pl.BlockSpec((1, D), const), pl.BlockSpec((1, D), const),
                  pl.BlockSpec((D, 2 * at_w), const)],
        out_specs=[pl.BlockSpec((1, C, at_w), row), pl.BlockSpec((1, C, at_w), row)],
        out_shape=[jax.ShapeDtypeStruct((B, C, at_w), BF16), jax.ShapeDtypeStruct((B, C, at_w), BF16)],
        compiler_params=_cparams(1),
        name="ctx_kv",
    )(ctx, sh, sc, lng, lnb, w_kv)


def _attn_kernel(q_ref, k_ref, v_ref, lq1_ref, lk1_ref, lq2_ref, lk2_ref, g_ref, o_ref, *, lam_init, at_d):
    q = q_ref[0]
    k = k_ref[0]
    v = v_ref[0]
    lam = (jnp.exp(jnp.sum(lq1_ref[...] * lk1_ref[...], axis=-1, keepdims=True))
           - jnp.exp(jnp.sum(lq2_ref[...] * lk2_ref[...], axis=-1, keepdims=True)) + lam_init)
    lane = lax.broadcasted_iota(jnp.int32, q.shape, 1)
    zero = jnp.zeros_like(q)
    nt = (((1,), (1,)), ((), ()))

    def softmax_part(qc):
        s = lax.dot_general(qc, k, nt, preferred_element_type=F32)
        p = jnp.exp(s - jnp.max(s, axis=-1, keepdims=True))
        return p, jnp.sum(p, axis=-1, keepdims=True)

    p1, l1 = softmax_part(jnp.where(lane < at_d, q, zero))
    p2, l2 = softmax_part(jnp.where(lane >= at_d, q, zero))
    a = p1 * (1.0 / l1) - p2 * (lam / l2)
    o = _dot(a.astype(v.dtype), v)
    o = o * lax.rsqrt(jnp.mean(o * o, axis=-1, keepdims=True) + RMS_EPS)
    o_ref[0] = (o * g_ref[...] * (1.0 - lam_init)).astype(o_ref.dtype)


def _attn(q, k_all, v_all, lq1, lk1, lq2, lk2, subln_g, lam_init, at_d, tq=256):
    B, S, W = q.shape
    Lk = k_all.shape[1]
    hw = 2 * at_d
    n_heads = W // hw
    qmap = lambda b, h, i: (b, i, h)
    kmap = lambda b, h, i: (b, 0, h)
    const = lambda b, h, i: (0, 0)
    return pl.pallas_call(
        functools.partial(_attn_kernel, lam_init=lam_init, at_d=at_d),
        grid=(B, n_heads, S // tq),
        in_specs=[pl.BlockSpec((1, tq, hw), qmap),
                  pl.BlockSpec((1, Lk, hw), kmap), pl.BlockSpec((1, Lk, hw), kmap),
                  pl.BlockSpec((1, at_d), const), pl.BlockSpec((1, at_d), const),
                  pl.BlockSpec((1, at_d), const), pl.BlockSpec((1, at_d), const),
                  pl.BlockSpec((1, hw), const)],
        out_specs=pl.BlockSpec((1, tq, hw), qmap),
        out_shape=jax.ShapeDtypeStruct((B, S, W), BF16),
        compiler_params=_cparams(3, VMEM_LIMIT),
        name="attn",
    )(q, k_all, v_all, lq1, lk1, lq2, lk2, subln_g)


def _dwconv_kernel(z_ref, w_ref, b_ref, o_ref):
    z = z_ref[0].astype(F32)
    L = z.shape[0]
    row = lax.broadcasted_iota(jnp.int32, z.shape, 0)
    zm = jnp.where(row == 0, 0.0, pltpu.roll(z, 1, 0))
    zp = jnp.where(row == L - 1, 0.0, pltpu.roll(z, L - 1, 0))
    o_ref[0] = (zm * w_ref[0:1, :] + z * w_ref[1:2, :] + zp * w_ref[2:3, :] + b_ref[...]).astype(o_ref.dtype)


def _dwconv(z, w, b, tc=512):
    B, L, C = z.shape
    return pl.pallas_call(
        _dwconv_kernel,
        grid=(B, C // tc),
        in_specs=[pl.BlockSpec((1, L, tc), lambda b, j: (b, 0, j)),
                  pl.BlockSpec((CONV_W, tc), lambda b, j: (0, j)),
                  pl.BlockSpec((1, tc), lambda b, j: (0, j))],
        out_specs=pl.BlockSpec((1, L, tc), lambda b, j: (b, 0, j)),
        out_shape=jax.ShapeDtypeStruct((B, L, C), BF16),
        compiler_params=_cparams(2, VMEM_LIMIT),
        name="dwconv",
    )(z, w, b)


def _filt_kernel(z_ref, w1_ref, b1_ref, w2_ref, b2_ref, w3_ref, b3_ref, fr_ref, wo_ref, dec_ref,
                 e_ref, d_ref, nyq_ref, *, width, order):
    i = pl.program_id(0)
    hdot = lambda a, b: jnp.dot(a, b, preferred_element_type=F32, precision=HIGHEST)
    hdn = jnp.sin(fr_ref[0:1, :] * (hdot(z_ref[...], w1_ref[...]) + b1_ref[...]))
    hdn = jnp.sin(fr_ref[1:2, :] * (hdot(hdn, w2_ref[...]) + b2_ref[...]))
    hdn = jnp.sin(fr_ref[2:3, :] * (hdot(hdn, w3_ref[...]) + b3_ref[...]))
    h = hdot(hdn, wo_ref[...])
    dec = dec_ref[...]
    tl = dec.shape[0]
    row = lax.broadcasted_iota(jnp.int32, (tl, width), 0) + i * tl
    alt = (1 - 2 * (row & 1)).astype(F32)

    @pl.when(i == 0)
    def _():
        nyq_ref[...] = jnp.zeros_like(nyq_ref)

    for o in range(order):
        fwd = h[:, (2 * o) * width:(2 * o + 1) * width] * dec
        bwd = h[:, (2 * o + 1) * width:(2 * o + 2) * width] * dec
        bwd = jnp.where(row == 0, 0.0, bwd)
        e = fwd + bwd
        e_ref[:, o * width:(o + 1) * width] = e.astype(e_ref.dtype)
        d_ref[:, o * width:(o + 1) * width] = (bwd - fwd).astype(d_ref.dtype)
        nyq_ref[:, o * width:(o + 1) * width] += jnp.sum(alt * e, axis=0, keepdims=True)


def _filt(zemb, w1, b1, w2, b2, w3, b3, freq, wout, decay, width, order, tl=512):
    L, E = zemb.shape
    F = w2.shape[0]
    const = lambda i: (0, 0)
    rowm = lambda i: (i, 0)
    return pl.pallas_call(
        functools.partial(_filt_kernel, width=width, order=order),
        grid=(L // tl,),
        in_specs=[pl.BlockSpec((tl, E), rowm),
                  pl.BlockSpec((E, F), const), pl.BlockSpec((1, F), const),
                  pl.BlockSpec((F, F), const), pl.BlockSpec((1, F), const),
                  pl.BlockSpec((F, F), const), pl.BlockSpec((1, F), const),
                  pl.BlockSpec((3, F), const),
                  pl.BlockSpec((F, order * 2 * width), const),
                  pl.BlockSpec((tl, width), rowm)],
        out_specs=[pl.BlockSpec((tl, order * width), rowm), pl.BlockSpec((tl, order * width), rowm),
                   pl.BlockSpec((1, order * width), const)],
        out_shape=[jax.ShapeDtypeStruct((L, order * width), BF16), jax.ShapeDtypeStruct((L, order * width), BF16),
                   jax.ShapeDtypeStruct((1, order * width), F32)],
        compiler_params=_cparams(1, VMEM_LIMIT),
        name="filt",
    )(zemb, w1, b1, w2, b2, w3, b3, freq, wout, decay)


def _fspec_kernel(c_ref, s_ref, e_ref, d_ref, kr_ref, ki_ref, *, n_fft):
    j = pl.program_id(0)
    tk = c_ref.shape[0]
    row = lax.broadcasted_iota(jnp.int32, kr_ref.shape, 0) + j * tk
    wk = jnp.where(row == 0, 1.0 / n_fft, 2.0 / n_fft)
    kr_ref[...] = _dot(c_ref[...], e_ref[...]) * wk
    ki_ref[...] = _dot(s_ref[...], d_ref[...]) * wk


def _fspec(cq, sq, e, d, tk=256):
    L, W = e.shape
    const = lambda j: (0, 0)
    rowm = lambda j: (j, 0)
    return pl.pallas_call(
        functools.partial(_fspec_kernel, n_fft=2 * L),
        grid=(L // tk,),
        in_specs=[pl.BlockSpec((tk, L), rowm), pl.BlockSpec((tk, L), rowm),
                  pl.BlockSpec((L, W), const, pipeline_mode=pl.Buffered(1)),
                  pl.BlockSpec((L, W), const, pipeline_mode=pl.Buffered(1))],
        out_specs=[pl.BlockSpec((tk, W), rowm), pl.BlockSpec((tk, W), rowm)],
        out_shape=[jax.ShapeDtypeStruct((L, W), F32), jax.ShapeDtypeStruct((L, W), F32)],
        compiler_params=_cparams(1, VMEM_LIMIT),
        name="fspec",
    )(cq, sq, e, d)


def _lconv_kernel(u_ref, gate_ref, crow_ref, srow_ref, ccol_ref, scol_ref, kr_ref, ki_ref, nyq_ref, bias_ref,
                  o_ref, acc_ref, *, n_fft):
    j = pl.program_id(1)

    @pl.when(j == 0)
    def _():
        acc_ref[...] = jnp.zeros_like(acc_ref)

    u = u_ref[0]
    xr = _dot(crow_ref[...], u)
    p = _dot(srow_ref[...], u)
    kr, ki = kr_ref[...], ki_ref[...]
    yr = (xr * kr + p * ki).astype(BF16)
    yi = (xr * ki - p * kr).astype(BF16)
    acc_ref[...] += _dot(ccol_ref[...], yr) - _dot(scol_ref[...], yi)

    @pl.when(j == pl.num_programs(1) - 1)
    def _():
        uf = u.astype(F32)
        row = lax.broadcasted_iota(jnp.int32, uf.shape, 0)
        alt = (1 - 2 * (row & 1)).astype(F32)
        x_nyq = jnp.sum(alt * uf, axis=0, keepdims=True)
        y = acc_ref[...] + alt * (x_nyq * nyq_ref[...] * (1.0 / n_fft))
        o_ref[0] = (gate_ref[0].astype(F32) * (y + uf * bias_ref[...])).astype(o_ref.dtype)


def _lconv(u_arr, u_col, gate_arr, gate_col, cq, sq, kr, ki, nyq, bias, order_idx, width, tk=128):
    B, L, _ = u_arr.shape
    return pl.pallas_call(
        functools.partial(_lconv_kernel, n_fft=2 * L),
        grid=(B, L // tk),
        in_specs=[pl.BlockSpec((1, L, width), lambda b, j: (b, 0, u_col), pipeline_mode=pl.Buffered(1)),
                  pl.BlockSpec((1, L, width), lambda b, j: (b, 0, gate_col), pipeline_mode=pl.Buffered(1)),
                  pl.BlockSpec((tk, L), lambda b, j: (j, 0)), pl.BlockSpec((tk, L), lambda b, j: (j, 0)),
                  pl.BlockSpec((L, tk), lambda b, j: (0, j)), pl.BlockSpec((L, tk), lambda b, j: (0, j)),
                  pl.BlockSpec((tk, width), lambda b, j: (j, order_idx)),
                  pl.BlockSpec((tk, width), lambda b, j: (j, order_idx)),
                  pl.BlockSpec((1, width), lambda b, j: (0, order_idx)),
                  pl.BlockSpec((1, width), lambda b, j: (0, 0))],
        out_specs=pl.BlockSpec((1, L, width), lambda b, j: (b, 0, 0)),
        out_shape=jax.ShapeDtypeStruct((B, L, width), BF16),
        scratch_shapes=[pltpu.VMEM((L, width), F32)],
        compiler_params=_cparams(2, VMEM_LIMIT),
        name="lconv%d" % order_idx,
    )(u_arr, gate_arr, cq, sq, cq, sq, kr, ki, nyq, bias)


def _merge_kernel(x_ref, yhy_ref, oat_ref, g_ref, g1_ref, lng_ref, lnb_ref, l1g_ref, l1b_ref,
                  whyo_ref, wato_ref, wout_ref, o_ref, *, alpha):
    d = whyo_ref.shape[1]
    xln = _layer_norm(x_ref[0], lng_ref[...], lnb_ref[...])
    g = g_ref[0].astype(F32)
    m = (_sigmoid(g[:, :d]) * _dot(yhy_ref[0], whyo_ref[...])
         + _sigmoid(g[:, d:]) * _dot(oat_ref[0], wato_ref[...]))
    y = _dot(m.astype(BF16), wout_ref[...])
    o_ref[0] = _layer_norm(alpha * xln + g1_ref[0] * y, l1g_ref[...], l1b_ref[...])


def _merge(x, yhy, oat, g, g1, lng, lnb, l1g, l1b, whyo, wato, wout, alpha, tm=512):
    B, S, D = x.shape
    row = lambda b, i: (b, i, 0)
    mod = lambda b, i: (b, 0, 0)
    const = lambda b, i: (0, 0)
    vec = pl.BlockSpec((1, D), const)
    return pl.pallas_call(
        functools.partial(_merge_kernel, alpha=alpha),
        grid=(B, S // tm),
        in_specs=[pl.BlockSpec((1, tm, D), row),
                  pl.BlockSpec((1, tm, yhy.shape[2]), row), pl.BlockSpec((1, tm, oat.shape[2]), row),
                  pl.BlockSpec((1, tm, g.shape[2]), row),
                  pl.BlockSpec((1, 1, D), mod), vec, vec, vec, vec,
                  pl.BlockSpec(whyo.shape, const), pl.BlockSpec(wato.shape, const), pl.BlockSpec(wout.shape, const)],
        out_specs=pl.BlockSpec((1, tm, D), row),
        out_shape=jax.ShapeDtypeStruct((B, S, D), F32),
        compiler_params=_cparams(2, VMEM_LIMIT),
        name="merge",
    )(x, yhy, oat, g, g1, lng, lnb, l1g, l1b, whyo, wato, wout)


def _ffn_kernel(x_ref, xp_ref, xn_ref, sh_ref, sc_ref, g2_ref, wup_ref, cw_ref, cb_ref, wdn_ref,
                l2g_ref, l2b_ref, o_ref, *, alpha, d_ff, tc, halo):
    i = pl.program_id(1)
    n_i = pl.num_programs(1)
    x = x_ref[0]
    tm = x.shape[0]
    sc = 1.0 + sc_ref[0]
    sh = sh_ref[0]
    hp = jnp.where(i > 0, xp_ref[0] * sc + sh, 0.0)
    hn = jnp.where(i < n_i - 1, xn_ref[0] * sc + sh, 0.0)
    h = jnp.concatenate([hp, x * sc + sh, hn], axis=0).astype(BF16)
    rows = tm + 2 * halo

    def conv(u, lo):
        w = cw_ref[:, lo:lo + tc]
        um = pltpu.roll(u, 1, 0)[halo:halo + tm]
        up = pltpu.roll(u, rows - 1, 0)[halo:halo + tm]
        return um * w[0:1] + u[halo:halo + tm] * w[1:2] + up * w[2:3] + cb_ref[:, lo:lo + tc]

    acc = jnp.zeros((tm, o_ref.shape[2]), F32)
    for c in range(d_ff // tc):
        a = conv(_dot(h, wup_ref[:, c * tc:(c + 1) * tc]), c * tc)
        g = conv(_dot(h, wup_ref[:, d_ff + c * tc:d_ff + (c + 1) * tc]), d_ff + c * tc)
        act = (g * _sigmoid(g) * a).astype(BF16)
        acc = acc + _dot(act, wdn_ref[c * tc:(c + 1) * tc, :])
    o_ref[0] = _layer_norm(alpha * x + g2_ref[0] * acc, l2g_ref[...], l2b_ref[...])


def _ffn(x1, sh, sc, g2, wup, cw, cb, wdn, l2g, l2b, alpha, tm=512, tc=256):
    B, S, D = x1.shape
    d_ff = wdn.shape[0]
    halo = V7X_SUBLANES
    nb = tm // halo
    row = lambda b, i: (b, i, 0)
    prev = lambda b, i: (b, jnp.maximum(i * nb - 1, 0), 0)
    nxt = lambda b, i: (b, jnp.minimum((i + 1) * nb, S // halo - 1), 0)
    mod = lambda b, i: (b, 0, 0)
    const = lambda b, i: (0, 0)
    vec = pl.BlockSpec((1, D), const)
    return pl.pallas_call(
        functools.partial(_ffn_kernel, alpha=alpha, d_ff=d_ff, tc=tc, halo=halo),
        grid=(B, S // tm),
        in_specs=[pl.BlockSpec((1, tm, D), row), pl.BlockSpec((1, halo, D), prev), pl.BlockSpec((1, halo, D), nxt),
                  pl.BlockSpec((1, 1, D), mod), pl.BlockSpec((1, 1, D), mod), pl.BlockSpec((1, 1, D), mod),
                  pl.BlockSpec(wup.shape, const, pipeline_mode=pl.Buffered(1)),
                  pl.BlockSpec(cw.shape, const), pl.BlockSpec(cb.shape, const),
                  pl.BlockSpec(wdn.shape, const, pipeline_mode=pl.Buffered(1)),
                  vec, vec],
        out_specs=pl.BlockSpec((1, tm, D), row),
        out_shape=jax.ShapeDtypeStruct((B, S, D), F32),
        compiler_params=_cparams(2, VMEM_LIMIT),
        name="ffn",
    )(x1, x1, x1, sh, sc, g2, wup, cw, cb, wdn, l2g, l2b)


def _rope_tables(S, at_d):
    rope_axis = at_d // 2
    half = rope_axis // 2
    pos = jnp.arange(S, dtype=jnp.int32)
    rowp = (pos // GRID_W).astype(F32)
    colp = (pos % GRID_W).astype(F32)
    inv = ROPE_BASE ** (-jnp.arange(0, rope_axis, 2, dtype=F32) / rope_axis)
    lane = jnp.arange(V7X_LANES)
    d = lane % at_d
    e = d % rope_axis
    f = e % half
    ang = jnp.where((d < rope_axis)[None, :], rowp[:, None], colp[:, None]) * inv[f][None, :]
    first = (e < half)[None, :]
    sin = jnp.sin(ang)
    return jnp.cos(ang), jnp.where(first, -sin, 0.0), jnp.where(first, 0.0, sin)


def _filter_tables(L, emb, width):
    bands = (emb - 1) // 2
    t = jnp.linspace(0.0, 1.0, L, dtype=F32)[:, None]
    w = (2.0 * math.pi / L) * jnp.arange(L, dtype=F32)[:, None]
    f = jnp.linspace(1e-4, bands - 1, bands, dtype=F32)[None, :]
    z = jnp.concatenate([t, jnp.cos(f * w), -jnp.sin(f * w)], -1)
    min_decay = math.log(HY_DECAY_TARGET) / HY_SLOW_PCT
    max_decay = math.log(HY_DECAY_TARGET) / HY_FAST_PCT
    deltas = jnp.linspace(min_decay, max_decay, width, dtype=F32)
    decay = jnp.exp(-t * jnp.abs(deltas)[None, :])
    return z, decay


def _dft_matrices(L):
    n_fft = 2 * L
    r = int(math.isqrt(L))
    assert r * r == L
    n = jnp.arange(L, dtype=jnp.int32)[None, :]
    ka = jnp.arange(r, dtype=jnp.int32)[:, None]
    scale = 2.0 * math.pi / n_fft
    pa = (((r * ka) * n) & (n_fft - 1)).astype(F32) * scale
    pb = ((ka * n) & (n_fft - 1)).astype(F32) * scale
    ca, sa = jnp.cos(pa)[:, None, :], jnp.sin(pa)[:, None, :]
    cb, sb = jnp.cos(pb)[None, :, :], jnp.sin(pb)[None, :, :]
    cq = (ca * cb - sa * sb).reshape(L, L).astype(BF16)
    sq = (sa * cb + ca * sb).reshape(L, L).astype(BF16)
    return cq, sq


def kernel(x, c, ctx, c_ctx, ln_in_g, ln_in_b, w_ada, b_ada, w_in, hy_conv_w, hy_conv_b, hy_f_w1, hy_f_b1, hy_f_w2, hy_f_b2, hy_f_w3, hy_f_b3, hy_f_freq, hy_f_wout, hy_bias, lam_q1, lam_k1, lam_q2, lam_k2, at_subln_g, w_hy_o, w_at_o, w_out, ln1_g, ln1_b, ffn_w_up, ffn_conv_w, ffn_conv_b, ffn_w_down, ln2_g, ln2_b):
    B, S, D = x.shape
    depth = w_ada.shape[0]
    assert depth == 1, "single-layer configuration only"
    l = 0
    order, width = hy_bias.shape[1], hy_bias.shape[2]
    at_d = lam_q1.shape[1]
    at_w = w_at_o.shape[1]
    hy_cols = (order + 1) * width
    emb = hy_f_w1.shape[1]
    alpha = (2.0 * depth) ** 0.25
    lam_init = 0.8 - 0.6 * math.exp(-0.3 * l)
    assert 2 * at_d == V7X_LANES and order == 2

    row2 = lambda a: a.reshape(1, -1)

    pad = (-(B + 1)) % V7X_SUBLANES
    cc = jnp.concatenate([c, c_ctx[None, :], jnp.zeros((pad, D), F32)], 0)
    mod = _ada(cc, w_ada[l], row2(b_ada[l]))
    sh1, sc1, g1, sh2, sc2, g2 = [mod[:B, i * D:(i + 1) * D].reshape(B, 1, D) for i in range(N_ADA)]
    sh1c, sc1c = mod[B:B + 1, 0:D], mod[B:B + 1, D:2 * D]

    lng, lnb = row2(ln_in_g), row2(ln_in_b)
    w_in_b = w_in[l].astype(BF16)
    cos, sin_a, sin_b = _rope_tables(S, at_d)
    z_hy, q, k_l, v_l, gates = _inproj(x, sh1, sc1, lng, lnb, w_in_b, cos, sin_a, sin_b, hy_cols, at_w, at_d)
    k_c, v_c = _ctx_kv(ctx, sh1c, sc1c, lng, lnb, w_in_b[:, hy_cols + at_w:hy_cols + 3 * at_w], at_w)
    k_all = jnp.concatenate([k_c, k_l], 1)
    v_all = jnp.concatenate([v_c, v_l], 1)
    o_at = _attn(q, k_all, v_all, row2(lam_q1[l]), row2(lam_k1[l]), row2(lam_q2[l]), row2(lam_k2[l]),
                 row2(at_subln_g[l]), lam_init, at_d)

    u3 = _dwconv(z_hy, hy_conv_w[l], row2(hy_conv_b[l]))
    zemb, decay = _filter_tables(S, emb, width)
    epad = (-emb) % V7X_LANES
    zemb = jnp.pad(zemb, ((0, 0), (0, epad)))
    w1p = jnp.pad(hy_f_w1[l], ((0, epad), (0, 0)))
    e_f, d_f, nyq = _filt(zemb, w1p, row2(hy_f_b1[l]), hy_f_w2[l], row2(hy_f_b2[l]), hy_f_w3[l],
                          row2(hy_f_b3[l]), hy_f_freq[l], hy_f_wout[l], decay, width, order)
    cq, sq = _dft_matrices(S)
    kr, ki = _fspec(cq, sq, e_f, d_f)
    zz = _lconv(u3, 0, u3, 1, cq, sq, kr, ki, nyq, row2(hy_bias[l, 0]), 0, width)
    y_hy = _lconv(zz, 0, u3, 2, cq, sq, kr, ki, nyq, row2(hy_bias[l, 1]), 1, width)

    x1 = _merge(x, y_hy, o_at, gates, g1, lng, lnb, row2(ln1_g[l]), row2(ln1_b[l]),
                w_hy_o[l].astype(BF16), w_at_o[l].astype(BF16), w_out[l].astype(BF16), alpha)
    return _ffn(x1, sh2, sc2, g2, ffn_w_up[l].astype(BF16), ffn_conv_w[l], row2(ffn_conv_b[l]),
                ffn_w_down[l].astype(BF16), row2(ln2_g[l]), row2(ln2_b[l]), alpha)
```

```python
import functools
import math

import numpy as np
import jax
import jax.numpy as jnp
from jax import lax
from jax.experimental import pallas as pl
from jax.experimental.pallas import tpu as pltpu

F32 = jnp.float32
BF16 = jnp.bfloat16

LN_EPS = 1e-5
RMS_EPS = 1e-5
GRID_W = 64
ROPE_BASE = 10000.0
N_ADA = 6
HY_DECAY_TARGET = 1e-2
HY_FAST_PCT = 0.3
HY_SLOW_PCT = 1.5
CONV_W = 3

V7X_LANES = 128
V7X_SUBLANES = 8
V7X_VMEM_BYTES = 64 * 1024 * 1024
VMEM_LIMIT = 56 * 1024 * 1024

HIGHEST = lax.Precision.HIGHEST


def _cparams(n_axes, vmem=None):
    return pltpu.CompilerParams(dimension_semantics=("arbitrary",) * n_axes, vmem_limit_bytes=vmem)


def _layer_norm(x, g, b):
    mu = jnp.mean(x, axis=-1, keepdims=True)
    xc = x - mu
    var = jnp.mean(xc * xc, axis=-1, keepdims=True)
    return xc * lax.rsqrt(var + LN_EPS) * g + b


def _sigmoid(x):
    return 1.0 / (1.0 + jnp.exp(-x))


def _dot(a, b):
    return jnp.dot(a, b, preferred_element_type=F32)


def _ada_kernel(c_ref, w_ref, b_ref, o_ref):
    c = c_ref[...]
    s = c * _sigmoid(c)
    o_ref[...] = jnp.dot(s, w_ref[...], preferred_element_type=F32, precision=HIGHEST) + b_ref[...]


def _ada(cc, w, b, tn=1536):
    rows, d = cc.shape
    n = w.shape[1]
    return pl.pallas_call(
        _ada_kernel,
        grid=(n // tn,),
        in_specs=[pl.BlockSpec((rows, d), lambda j: (0, 0)),
                  pl.BlockSpec((d, tn), lambda j: (0, j)),
                  pl.BlockSpec((1, tn), lambda j: (0, j))],
        out_specs=pl.BlockSpec((rows, tn), lambda j: (0, j)),
        out_shape=jax.ShapeDtypeStruct((rows, n), F32),
        compiler_params=_cparams(1, VMEM_LIMIT),
        name="ada",
    )(cc, w, b)


def _rope(x, cos, sin_a, sin_b, n_heads):
    outs = []
    for h in range(n_heads):
        xh = x[:, h * V7X_LANES:(h + 1) * V7X_LANES]
        up = pltpu.roll(xh, V7X_LANES - 16, 1)
        dn = pltpu.roll(xh, 16, 1)
        outs.append(xh * cos + up * sin_a + dn * sin_b)
    return jnp.concatenate(outs, axis=1)


def _inproj_kernel(x_ref, sh_ref, sc_ref, lng_ref, lnb_ref, w_ref, cos_ref, sa_ref, sb_ref,
                   hy_ref, q_ref, k_ref, v_ref, g_ref, *, hy_cols, at_w, q_scale):
    xn = _layer_norm(x_ref[0], lng_ref[...], lnb_ref[...])
    h = (xn * (1.0 + sc_ref[0]) + sh_ref[0]).astype(BF16)
    n_heads = at_w // V7X_LANES
    o1 = hy_cols
    o2, o3, o4 = o1 + at_w, o1 + 2 * at_w, o1 + 3 * at_w
    hy_ref[0] = _dot(h, w_ref[:, 0:o1]).astype(hy_ref.dtype)
    cos, sa, sb = cos_ref[...], sa_ref[...], sb_ref[...]
    q = _rope(_dot(h, w_ref[:, o1:o2]), cos, sa, sb, n_heads)
    q_ref[0] = (q * q_scale).T.astype(q_ref.dtype)
    k = _rope(_dot(h, w_ref[:, o2:o3]), cos, sa, sb, n_heads)
    k_ref[0] = k.astype(k_ref.dtype)
    v_ref[0] = _dot(h, w_ref[:, o3:o4]).T.astype(v_ref.dtype)
    g_ref[0] = _dot(h, w_ref[:, o4:]).astype(g_ref.dtype)


def _inproj(x, sh, sc, lng, lnb, w, cos, sa, sb, hy_cols, at_w, at_d, tm=512):
    B, S, D = x.shape
    ncols = w.shape[1]
    g_cols = ncols - hy_cols - 3 * at_w
    kern = functools.partial(_inproj_kernel, hy_cols=hy_cols, at_w=at_w, q_scale=at_d ** -0.5 * math.log2(math.e))
    row = lambda i, b: (b, i, 0)
    rowt = lambda i, b: (b, 0, i)
    mod = lambda i, b: (b, 0, 0)
    const = lambda i, b: (0, 0)
    tab = lambda i, b: (i, 0)
    return pl.pallas_call(
        kern,
        grid=(S // tm, B),
        in_specs=[pl.BlockSpec((1, tm, D), row),
                  pl.BlockSpec((1, 1, D), mod), pl.BlockSpec((1, 1, D), mod),
                  pl.BlockSpec((1, D), const), pl.BlockSpec((1, D), const),
                  pl.BlockSpec((D, ncols), const, pipeline_mode=pl.Buffered(1)),
                  pl.BlockSpec((tm, V7X_LANES), tab), pl.BlockSpec((tm, V7X_LANES), tab),
                  pl.BlockSpec((tm, V7X_LANES), tab)],
        out_specs=[pl.BlockSpec((1, tm, hy_cols), row), pl.BlockSpec((1, at_w, tm), rowt),
                   pl.BlockSpec((1, tm, at_w), row), pl.BlockSpec((1, at_w, tm), rowt),
                   pl.BlockSpec((1, tm, g_cols), row)],
        out_shape=[jax.ShapeDtypeStruct((B, S, hy_cols), BF16), jax.ShapeDtypeStruct((B, at_w, S), BF16),
                   jax.ShapeDtypeStruct((B, S, at_w), BF16), jax.ShapeDtypeStruct((B, at_w, S), BF16),
                   jax.ShapeDtypeStruct((B, S, g_cols), BF16)],
        compiler_params=_cparams(2, VMEM_LIMIT),
        name="inproj",
    )(x, sh, sc, lng, lnb, w, cos, sa, sb)


def _ctx_kv_kernel(x_ref, sh_ref, sc_ref, lng_ref, lnb_ref, w_ref, k_ref, v_ref, *, at_w):
    xn = _layer_norm(x_ref[0], lng_ref[...], lnb_ref[...])
    h = (xn * (1.0 + sc_ref[...]) + sh_ref[...]).astype(BF16)
    k_ref[0] = _dot(h, w_ref[:, 0:at_w]).astype(k_ref.dtype)
    v_ref[0] = _dot(h, w_ref[:, at_w:]).T.astype(v_ref.dtype)


def _ctx_kv(ctx, sh, sc, lng, lnb, w_kv, at_w):
    B, C, D = ctx.shape
    const = lambda b: (0, 0)
    row = lambda b: (b, 0, 0)
    return pl.pallas_call(
        functools.partial(_ctx_kv_kernel, at_w=at_w),
        grid=(B,),
        in_specs=[pl.BlockSpec((1, C, D), row),
                  pl.BlockSpec((1, D), const), pl.BlockSpec((1, D), const),
                  pl.BlockSpec((1, D), const), pl.BlockSpec((1, D), const),
                  pl.BlockSpec((D, 2 * at_w), const)],
        out_specs=[pl.BlockSpec((1, C, at_w), row), pl.BlockSpec((1, at_w, C), row)],
        out_shape=[jax.ShapeDtypeStruct((B, C, at_w), BF16), jax.ShapeDtypeStruct((B, at_w, C), BF16)],
        compiler_params=_cparams(1),
        name="ctx_kv",
    )(ctx, sh, sc, lng, lnb, w_kv)


def _attn_kernel(qt_ref, k_ref, vt_ref, lq1_ref, lk1_ref, lq2_ref, lk2_ref, g_ref, o_ref, *,
                 lam_init, at_d, chunks):
    qt = qt_ref[0]
    hw, tq = qt.shape
    lam = (jnp.exp(jnp.sum(lq1_ref[...] * lk1_ref[...], axis=-1, keepdims=True))
           - jnp.exp(jnp.sum(lq2_ref[...] * lk2_ref[...], axis=-1, keepdims=True)) + lam_init)
    row = lax.broadcasted_iota(jnp.int32, qt.shape, 0)
    zero = jnp.zeros_like(qt)
    qts = (jnp.where(row < at_d, qt, zero), jnp.where(row >= at_d, qt, zero))
    state = [(jnp.full((1, tq), -1e30, F32), jnp.zeros((1, tq), F32), jnp.zeros((hw, tq), F32))
             for _ in range(2)]
    units = [(off, kc, comp) for off, kc in chunks for comp in range(2)]

    def scores(unit):
        off, kc, comp = unit
        return _dot(k_ref[0, off:off + kc, :], qts[comp])

    st = scores(units[0])
    for i, (off, kc, comp) in enumerate(units):
        st_next = scores(units[i + 1]) if i + 1 < len(units) else None
        m, l, acc = state[comp]
        m_new = jnp.maximum(m, jnp.max(st, axis=0, keepdims=True))
        alpha = jnp.exp2(m - m_new)
        p = jnp.exp2(st - m_new)
        l = alpha * l + jnp.sum(p, axis=0, keepdims=True)
        acc = alpha * acc + _dot(vt_ref[0, :, off:off + kc], p.astype(BF16))
        state[comp] = (m_new, l, acc)
        st = st_next
    (_, l1, a1), (_, l2, a2) = state
    ot = a1 * (1.0 / l1) - a2 * (lam / l2)
    ot = ot * lax.rsqrt(jnp.mean(ot * ot, axis=0, keepdims=True) + RMS_EPS)
    o_ref[0] = (ot.T * g_ref[...] * (1.0 - lam_init)).astype(o_ref.dtype)


def _attn(qt, k_all, vt_all, lq1, lk1, lq2, lk2, subln_g, lam_init, at_d, n_ctx, tq=512, kc=1024):
    B, W, S = qt.shape
    Lk = k_all.shape[1]
    hw = 2 * at_d
    n_heads = W // hw
    assert (Lk - n_ctx) % kc == 0 and n_ctx % V7X_LANES == 0
    chunks = ((0, n_ctx),) + tuple((n_ctx + i * kc, kc) for i in range((Lk - n_ctx) // kc))
    qmap = lambda b, h, i: (b, h, i)
    omap = lambda b, h, i: (b, i, h)
    const = lambda b, h, i: (0, 0)
    return pl.pallas_call(
        functools.partial(_attn_kernel, lam_init=lam_init, at_d=at_d, chunks=chunks),
        grid=(B, n_heads, S // tq),
        in_specs=[pl.BlockSpec((1, hw, tq), qmap),
                  pl.BlockSpec((1, Lk, hw), lambda b, h, i: (b, 0, h)),
                  pl.BlockSpec((1, hw, Lk), lambda b, h, i: (b, h, 0)),
                  pl.BlockSpec((1, at_d), const), pl.BlockSpec((1, at_d), const),
                  pl.BlockSpec((1, at_d), const), pl.BlockSpec((1, at_d), const),
                  pl.BlockSpec((1, hw), const)],
        out_specs=pl.BlockSpec((1, tq, hw), omap),
        out_shape=jax.ShapeDtypeStruct((B, S, W), BF16),
        compiler_params=_cparams(3, VMEM_LIMIT),
        name="attn",
    )(qt, k_all, vt_all, lq1, lk1, lq2, lk2, subln_g)


def _dwconv_kernel(z_ref, w_ref, b_ref, o_ref):
    z = z_ref[0].astype(F32)
    L = z.shape[0]
    row = lax.broadcasted_iota(jnp.int32, z.shape, 0)
    zm = jnp.where(row == 0, 0.0, pltpu.roll(z, 1, 0))
    zp = jnp.where(row == L - 1, 0.0, pltpu.roll(z, L - 1, 0))
    o_ref[0] = (zm * w_ref[0:1, :] + z * w_ref[1:2, :] + zp * w_ref[2:3, :] + b_ref[...]).astype(o_ref.dtype)


def _dwconv(z, w, b, tc=512):
    B, L, C = z.shape
    return pl.pallas_call(
        _dwconv_kernel,
        grid=(B, C // tc),
        in_specs=[pl.BlockSpec((1, L, tc), lambda b, j: (b, 0, j)),
                  pl.BlockSpec((CONV_W, tc), lambda b, j: (0, j)),
                  pl.BlockSpec((1, tc), lambda b, j: (0, j))],
        out_specs=pl.BlockSpec((1, L, tc), lambda b, j: (b, 0, j)),
        out_shape=jax.ShapeDtypeStruct((B, L, C), BF16),
        compiler_params=_cparams(2, VMEM_LIMIT),
        name="dwconv",
    )(z, w, b)


def _filt_kernel(z_ref, w1_ref, b1_ref, w2_ref, b2_ref, w3_ref, b3_ref, fr_ref, wo_ref, dec_ref,
                 f_ref, b_ref, *, width, order):
    i = pl.program_id(0)
    hdot = lambda a, b: jnp.dot(a, b, preferred_element_type=F32, precision=HIGHEST)
    hdn = jnp.sin(fr_ref[0:1, :] * (hdot(z_ref[...], w1_ref[...]) + b1_ref[...]))
    hdn = jnp.sin(fr_ref[1:2, :] * (hdot(hdn, w2_ref[...]) + b2_ref[...]))
    hdn = jnp.sin(fr_ref[2:3, :] * (hdot(hdn, w3_ref[...]) + b3_ref[...]))
    h = hdot(hdn, wo_ref[...])
    dec = dec_ref[...]
    tl = dec.shape[0]
    row = lax.broadcasted_iota(jnp.int32, (tl, width), 0) + i * tl
    for o in range(order):
        f_ref[:, o * width:(o + 1) * width] = h[:, (2 * o) * width:(2 * o + 1) * width] * dec
        bwd = h[:, (2 * o + 1) * width:(2 * o + 2) * width] * dec
        b_ref[:, o * width:(o + 1) * width] = jnp.where(row == 0, 0.0, bwd)


def _filt(zemb, w1, b1, w2, b2, w3, b3, freq, wout, decay, width, order, tl=512):
    L, E = zemb.shape
    F = w2.shape[0]
    const = lambda i: (0, 0)
    rowm = lambda i: (i, 0)
    return pl.pallas_call(
        functools.partial(_filt_kernel, width=width, order=order),
        grid=(L // tl,),
        in_specs=[pl.BlockSpec((tl, E), rowm),
                  pl.BlockSpec((E, F), const), pl.BlockSpec((1, F), const),
                  pl.BlockSpec((F, F), const), pl.BlockSpec((1, F), const),
                  pl.BlockSpec((F, F), const), pl.BlockSpec((1, F), const),
                  pl.BlockSpec((3, F), const),
                  pl.BlockSpec((F, order * 2 * width), const),
                  pl.BlockSpec((tl, width), rowm)],
        out_specs=[pl.BlockSpec((tl, order * width), rowm), pl.BlockSpec((tl, order * width), rowm)],
        out_shape=[jax.ShapeDtypeStruct((L, order * width), F32), jax.ShapeDtypeStruct((L, order * width), F32)],
        compiler_params=_cparams(1, VMEM_LIMIT),
        name="filt",
    )(zemb, w1, b1, w2, b2, w3, b3, freq, wout, decay)


FFT_NB = 64
FFT_GROUP = 8


def _ld(ref, start, size, stride=None):
    idx = pl.ds(start, size) if stride is None else pl.ds(start, size, stride=stride)
    return jnp.concatenate([ref[t, idx, :] for t in range(ref.shape[0])], axis=1)


def _st(ref, start, size, val, stride=None):
    idx = pl.ds(start, size) if stride is None else pl.ds(start, size, stride=stride)
    for t in range(ref.shape[0]):
        ref[t, idx, :] = val[:, t * V7X_LANES:(t + 1) * V7X_LANES]


def _fft_stage_a(src_ref, s_ref, fa, *, a_len, kp):
    def body(g, carry):
        for j in range(FFT_GROUP):
            r = g * FFT_GROUP + j
            ur = _ld(src_ref, r, a_len, FFT_NB).astype(BF16)
            _st(s_ref, pl.multiple_of(r * (2 * kp), V7X_SUBLANES), 2 * kp, _dot(fa, ur))
        return carry
    lax.fori_loop(0, FFT_NB // FFT_GROUP, body, 0)


def _fft_stage_c(s_ref, gc_ref, k1, kp):
    sr = _ld(s_ref, k1, FFT_NB, 2 * kp)
    si = _ld(s_ref, kp + k1, FFT_NB, 2 * kp)
    x = _dot(gc_ref[k1], jnp.concatenate([sr, si], axis=0).astype(BF16))
    return x[:FFT_NB], x[FFT_NB:]


def _fspec_kernel(f_ref, b_ref, fa_ref, gc_ref, kr_ref, ki_ref, uf_ref, s_ref, *, a_len, kp):
    fa = fa_ref[...]

    def transform(src_ref, emit):
        _st(uf_ref, 0, uf_ref.shape[1], src_ref[...])
        _fft_stage_a(uf_ref, s_ref, fa, a_len=a_len, kp=kp)

        def body(g, carry):
            for j in range(FFT_GROUP):
                k1 = g * FFT_GROUP + j
                xr, xi = _fft_stage_c(s_ref, gc_ref, k1, kp)
                emit(k1, xr, xi)
            return carry
        lax.fori_loop(0, kp // FFT_GROUP, body, 0)

    def emit_fwd(k1, xr, xi):
        kr_ref[k1] = xr
        ki_ref[k1] = xi

    def emit_bwd(k1, xr, xi):
        kr_ref[k1] += xr
        ki_ref[k1] -= xi

    transform(f_ref, emit_fwd)
    transform(b_ref, emit_bwd)


def _fspec(f_taps, b_taps, fa, gc, a_len, kp, tc=256):
    L, W = f_taps.shape
    const2 = lambda j: (0, 0)
    const3 = lambda j: (0, 0, 0)
    col = lambda j: (0, j)
    return pl.pallas_call(
        functools.partial(_fspec_kernel, a_len=a_len, kp=kp),
        grid=(W // tc,),
        in_specs=[pl.BlockSpec((L, tc), col), pl.BlockSpec((L, tc), col),
                  pl.BlockSpec(fa.shape, const2), pl.BlockSpec(gc.shape, const3)],
        out_specs=[pl.BlockSpec((kp, FFT_NB, tc), lambda j: (0, 0, j)),
                   pl.BlockSpec((kp, FFT_NB, tc), lambda j: (0, 0, j))],
        out_shape=[jax.ShapeDtypeStruct((kp, FFT_NB, W), F32), jax.ShapeDtypeStruct((kp, FFT_NB, W), F32)],
        scratch_shapes=[pltpu.VMEM((tc // V7X_LANES, L, V7X_LANES), F32),
                        pltpu.VMEM((tc // V7X_LANES, FFT_NB * 2 * kp, V7X_LANES), F32)],
        compiler_params=_cparams(1, VMEM_LIMIT),
        name="fspec",
    )(f_taps, b_taps, fa, gc)


def _lconv_kernel(u_ref, gate_ref, fa_ref, fai_ref, gc_ref, gci_ref, kr_ref, ki_ref, bias_ref, o_ref,
                  uf_ref, s_ref, *, a_len, kp):
    _st(uf_ref, 0, uf_ref.shape[1], u_ref[0].astype(F32))
    _fft_stage_a(uf_ref, s_ref, fa_ref[...], a_len=a_len, kp=kp)

    def spectrum_product(g, carry):
        k1s = [g * FFT_GROUP + j for j in range(FFT_GROUP)]
        xs = [_fft_stage_c(s_ref, gc_ref, k1, kp) for k1 in k1s]
        outs = []
        for k1, (xr, xi) in zip(k1s, xs):
            kr, ki = kr_ref[k1], ki_ref[k1]
            y = jnp.concatenate([xr * kr - xi * ki, xr * ki + xi * kr], axis=0).astype(BF16)
            outs.append(_dot(gci_ref[k1], y))
        for k1, bc in zip(k1s, outs):
            _st(s_ref, k1, FFT_NB, bc[:FFT_NB], 2 * kp)
            _st(s_ref, kp + k1, FFT_NB, bc[FFT_NB:], 2 * kp)
        return carry
    lax.fori_loop(0, kp // FFT_GROUP, spectrum_product, 0)

    fai = fai_ref[...]

    def inverse_a(g, carry):
        offs = [pl.multiple_of((g * FFT_GROUP + j) * (2 * kp), V7X_SUBLANES) for j in range(FFT_GROUP)]
        slabs = [_ld(s_ref, off, 2 * kp).astype(BF16) for off in offs]
        for off, slab in zip(offs, slabs):
            _st(s_ref, off, a_len, _dot(fai, slab))
        return carry
    lax.fori_loop(0, FFT_NB // FFT_GROUP, inverse_a, 0)

    bias = bias_ref[...]

    def epilogue(g, carry):
        for j in range(FFT_GROUP):
            a = g * FFT_GROUP + j
            r0 = pl.multiple_of(a * FFT_NB, FFT_NB)
            y = _ld(s_ref, a, FFT_NB, 2 * kp)
            uf = _ld(uf_ref, r0, FFT_NB)
            gate = gate_ref[0, pl.ds(r0, FFT_NB), :].astype(F32)
            o_ref[0, pl.ds(r0, FFT_NB), :] = (gate * (y + uf * bias)).astype(o_ref.dtype)
        return carry
    lax.fori_loop(0, a_len // FFT_GROUP, epilogue, 0)


def _lconv(u_arr, u_col, gate_arr, gate_col, fa, fai, gc, gci, kr, ki, bias, order_idx, width, a_len, kp, tc=256):
    B, L, _ = u_arr.shape
    nh = width // tc
    const2 = lambda h, b: (0, 0)
    const3 = lambda h, b: (0, 0, 0)
    spec = lambda h, b: (0, 0, order_idx * nh + h)
    return pl.pallas_call(
        functools.partial(_lconv_kernel, a_len=a_len, kp=kp),
        grid=(nh, B),
        in_specs=[pl.BlockSpec((1, L, tc), lambda h, b: (b, 0, u_col * nh + h)),
                  pl.BlockSpec((1, L, tc), lambda h, b: (b, 0, gate_col * nh + h)),
                  pl.BlockSpec(fa.shape, const2), pl.BlockSpec(fai.shape, const2),
                  pl.BlockSpec(gc.shape, const3, pipeline_mode=pl.Buffered(1)),
                  pl.BlockSpec(gci.shape, const3, pipeline_mode=pl.Buffered(1)),
                  pl.BlockSpec((kp, FFT_NB, tc), spec, pipeline_mode=pl.Buffered(1)),
                  pl.BlockSpec((kp, FFT_NB, tc), spec, pipeline_mode=pl.Buffered(1)),
                  pl.BlockSpec((1, tc), lambda h, b: (0, h))],
        out_specs=pl.BlockSpec((1, L, tc), lambda h, b: (b, 0, h)),
        out_shape=jax.ShapeDtypeStruct((B, L, width), BF16),
        scratch_shapes=[pltpu.VMEM((tc // V7X_LANES, L, V7X_LANES), F32),
                        pltpu.VMEM((tc // V7X_LANES, FFT_NB * 2 * kp, V7X_LANES), F32)],
        compiler_params=_cparams(2, VMEM_LIMIT),
        name="lconv%d" % order_idx,
    )(u_arr, gate_arr, fa, fai, gc, gci, kr, ki, bias)


def _merge_kernel(x_ref, yhy_ref, oat_ref, g_ref, g1_ref, lng_ref, lnb_ref, l1g_ref, l1b_ref,
                  whyo_ref, wato_ref, wout_ref, o_ref, *, alpha):
    d = whyo_ref.shape[1]
    xln = _layer_norm(x_ref[0], lng_ref[...], lnb_ref[...])
    g = g_ref[0].astype(F32)
    m = (_sigmoid(g[:, :d]) * _dot(yhy_ref[0], whyo_ref[...])
         + _sigmoid(g[:, d:]) * _dot(oat_ref[0], wato_ref[...]))
    y = _dot(m.astype(BF16), wout_ref[...])
    o_ref[0] = _layer_norm(alpha * xln + g1_ref[0] * y, l1g_ref[...], l1b_ref[...])


def _merge(x, yhy, oat, g, g1, lng, lnb, l1g, l1b, whyo, wato, wout, alpha, tm=512):
    B, S, D = x.shape
    row = lambda b, i: (b, i, 0)
    mod = lambda b, i: (b, 0, 0)
    const = lambda b, i: (0, 0)
    vec = pl.BlockSpec((1, D), const)
    return pl.pallas_call(
        functools.partial(_merge_kernel, alpha=alpha),
        grid=(B, S // tm),
        in_specs=[pl.BlockSpec((1, tm, D), row),
                  pl.BlockSpec((1, tm, yhy.shape[2]), row), pl.BlockSpec((1, tm, oat.shape[2]), row),
                  pl.BlockSpec((1, tm, g.shape[2]), row),
                  pl.BlockSpec((1, 1, D), mod), vec, vec, vec, vec,
                  pl.BlockSpec(whyo.shape, const), pl.BlockSpec(wato.shape, const), pl.BlockSpec(wout.shape, const)],
        out_specs=pl.BlockSpec((1, tm, D), row),
        out_shape=jax.ShapeDtypeStruct((B, S, D), F32),
        compiler_params=_cparams(2, VMEM_LIMIT),
        name="merge",
    )(x, yhy, oat, g, g1, lng, lnb, l1g, l1b, whyo, wato, wout)


def _ffn_kernel(x_ref, xp_ref, xn_ref, sh_ref, sc_ref, g2_ref, wup_ref, cw_ref, cb_ref, wdn_ref,
                l2g_ref, l2b_ref, o_ref, *, alpha, d_ff, tc, halo):
    i = pl.program_id(1)
    n_i = pl.num_programs(1)
    x = x_ref[0]
    tm = x.shape[0]
    sc = 1.0 + sc_ref[0]
    sh = sh_ref[0]
    hp = jnp.where(i > 0, xp_ref[0] * sc + sh, 0.0)
    hn = jnp.where(i < n_i - 1, xn_ref[0] * sc + sh, 0.0)
    h = jnp.concatenate([hp, x * sc + sh, hn], axis=0).astype(BF16)
    rows = tm + 2 * halo

    def conv(u, lo):
        w = cw_ref[:, lo:lo + tc]
        um = pltpu.roll(u, 1, 0)[halo:halo + tm]
        up = pltpu.roll(u, rows - 1, 0)[halo:halo + tm]
        return um * w[0:1] + u[halo:halo + tm] * w[1:2] + up * w[2:3] + cb_ref[:, lo:lo + tc]

    acc = jnp.zeros((tm, o_ref.shape[2]), F32)
    for c in range(d_ff // tc):
        a = conv(_dot(h, wup_ref[:, c * tc:(c + 1) * tc]), c * tc)
        g = conv(_dot(h, wup_ref[:, d_ff + c * tc:d_ff + (c + 1) * tc]), d_ff + c * tc)
        act = (g * _sigmoid(g) * a).astype(BF16)
        acc = acc + _dot(act, wdn_ref[c * tc:(c + 1) * tc, :])
    o_ref[0] = _layer_norm(alpha * x + g2_ref[0] * acc, l2g_ref[...], l2b_ref[...])


def _ffn(x1, sh, sc, g2, wup, cw, cb, wdn, l2g, l2b, alpha, tm=512, tc=256):
    B, S, D = x1.shape
    d_ff = wdn.shape[0]
    halo = V7X_SUBLANES
    nb = tm // halo
    row = lambda b, i: (b, i, 0)
    prev = lambda b, i: (b, jnp.maximum(i * nb - 1, 0), 0)
    nxt = lambda b, i: (b, jnp.minimum((i + 1) * nb, S // halo - 1), 0)
    mod = lambda b, i: (b, 0, 0)
    const = lambda b, i: (0, 0)
    vec = pl.BlockSpec((1, D), const)
    return pl.pallas_call(
        functools.partial(_ffn_kernel, alpha=alpha, d_ff=d_ff, tc=tc, halo=halo),
        grid=(B, S // tm),
        in_specs=[pl.BlockSpec((1, tm, D), row), pl.BlockSpec((1, halo, D), prev), pl.BlockSpec((1, halo, D), nxt),
                  pl.BlockSpec((1, 1, D), mod), pl.BlockSpec((1, 1, D), mod), pl.BlockSpec((1, 1, D), mod),
                  pl.BlockSpec(wup.shape, const, pipeline_mode=pl.Buffered(1)),
                  pl.BlockSpec(cw.shape, const), pl.BlockSpec(cb.shape, const),
                  pl.BlockSpec(wdn.shape, const, pipeline_mode=pl.Buffered(1)),
                  vec, vec],
        out_specs=pl.BlockSpec((1, tm, D), row),
        out_shape=jax.ShapeDtypeStruct((B, S, D), F32),
        compiler_params=_cparams(2, VMEM_LIMIT),
        name="ffn",
    )(x1, x1, x1, sh, sc, g2, wup, cw, cb, wdn, l2g, l2b)


def _rope_tables(S, at_d):
    rope_axis = at_d // 2
    half = rope_axis // 2
    pos = jnp.arange(S, dtype=jnp.int32)
    rowp = (pos // GRID_W).astype(F32)
    colp = (pos % GRID_W).astype(F32)
    inv = ROPE_BASE ** (-jnp.arange(0, rope_axis, 2, dtype=F32) / rope_axis)
    lane = jnp.arange(V7X_LANES)
    d = lane % at_d
    e = d % rope_axis
    f = e % half
    ang = jnp.where((d < rope_axis)[None, :], rowp[:, None], colp[:, None]) * inv[f][None, :]
    first = (e < half)[None, :]
    sin = jnp.sin(ang)
    return jnp.cos(ang), jnp.where(first, -sin, 0.0), jnp.where(first, 0.0, sin)


def _filter_tables(L, emb, width):
    bands = (emb - 1) // 2
    t = jnp.linspace(0.0, 1.0, L, dtype=F32)[:, None]
    w = (2.0 * math.pi / L) * jnp.arange(L, dtype=F32)[:, None]
    f = jnp.linspace(1e-4, bands - 1, bands, dtype=F32)[None, :]
    z = jnp.concatenate([t, jnp.cos(f * w), -jnp.sin(f * w)], -1)
    min_decay = math.log(HY_DECAY_TARGET) / HY_SLOW_PCT
    max_decay = math.log(HY_DECAY_TARGET) / HY_FAST_PCT
    deltas = jnp.linspace(min_decay, max_decay, width, dtype=F32)
    decay = jnp.exp(-t * jnp.abs(deltas)[None, :])
    return z, decay


def _fft_tables(L):
    a_len = L // FFT_NB
    assert a_len * FFT_NB == L and a_len % FFT_GROUP == 0
    n_fft, n1 = 2 * L, 2 * a_len
    kp = -(-(a_len + 1) // FFT_GROUP) * FFT_GROUP
    k1 = np.arange(kp)
    valid = (k1 <= a_len).astype(np.float64)
    ph = (np.outer(k1, np.arange(a_len)) % n1) * (2.0 * np.pi / n1)
    ca, sa = np.cos(ph) * valid[:, None], np.sin(ph) * valid[:, None]
    fa = np.concatenate([ca, -sa], axis=0)
    w = np.where((k1 == 0) | (k1 == a_len), 1.0, 2.0) / n_fft
    fai = np.concatenate([(ca * w[:, None]).T, (-sa * w[:, None]).T], axis=1)
    r = np.arange(FFT_NB)
    kk = k1[:, None, None] + n1 * r[None, :, None]
    th = ((kk * r[None, None, :]) % n_fft) * (2.0 * np.pi / n_fft)
    gr, gi = np.cos(th) * valid[:, None, None], -np.sin(th) * valid[:, None, None]
    gc = np.concatenate([np.concatenate([gr, -gi], 2), np.concatenate([gi, gr], 2)], 1)
    grt, git = gr.transpose(0, 2, 1), gi.transpose(0, 2, 1)
    gci = np.concatenate([np.concatenate([grt, git], 2), np.concatenate([-git, grt], 2)], 1)
    as_bf16 = lambda m: jnp.asarray(m, dtype=F32).astype(BF16)
    return a_len, kp, as_bf16(fa), as_bf16(fai), as_bf16(gc), as_bf16(gci)


def kernel(x, c, ctx, c_ctx, ln_in_g, ln_in_b, w_ada, b_ada, w_in, hy_conv_w, hy_conv_b, hy_f_w1, hy_f_b1, hy_f_w2, hy_f_b2, hy_f_w3, hy_f_b3, hy_f_freq, hy_f_wout, hy_bias, lam_q1, lam_k1, lam_q2, lam_k2, at_subln_g, w_hy_o, w_at_o, w_out, ln1_g, ln1_b, ffn_w_up, ffn_conv_w, ffn_conv_b, ffn_w_down, ln2_g, ln2_b):
    B, S, D = x.shape
    depth = w_ada.shape[0]
    assert depth == 1, "single-layer configuration only"
    l = 0
    order, width = hy_bias.shape[1], hy_bias.shape[2]
    at_d = lam_q1.shape[1]
    at_w = w_at_o.shape[1]
    hy_cols = (order + 1) * width
    emb = hy_f_w1.shape[1]
    alpha = (2.0 * depth) ** 0.25
    lam_init = 0.8 - 0.6 * math.exp(-0.3 * l)
    assert 2 * at_d == V7X_LANES and order == 2

    row2 = lambda a: a.reshape(1, -1)

    pad = (-(B + 1)) % V7X_SUBLANES
    cc = jnp.concatenate([c, c_ctx[None, :], jnp.zeros((pad, D), F32)], 0)
    mod = _ada(cc, w_ada[l], row2(b_ada[l]))
    sh1, sc1, g1, sh2, sc2, g2 = [mod[:B, i * D:(i + 1) * D].reshape(B, 1, D) for i in range(N_ADA)]
    sh1c, sc1c = mod[B:B + 1, 0:D], mod[B:B + 1, D:2 * D]

    lng, lnb = row2(ln_in_g), row2(ln_in_b)
    w_in_b = w_in[l].astype(BF16)
    cos, sin_a, sin_b = _rope_tables(S, at_d)
    z_hy, qt, k_l, vt_l, gates = _inproj(x, sh1, sc1, lng, lnb, w_in_b, cos, sin_a, sin_b, hy_cols, at_w, at_d)
    k_c, vt_c = _ctx_kv(ctx, sh1c, sc1c, lng, lnb, w_in_b[:, hy_cols + at_w:hy_cols + 3 * at_w], at_w)
    k_all = jnp.concatenate([k_c, k_l], 1)
    vt_all = jnp.concatenate([vt_c, vt_l], 2)
    o_at = _attn(qt, k_all, vt_all, row2(lam_q1[l]), row2(lam_k1[l]), row2(lam_q2[l]), row2(lam_k2[l]),
                 row2(at_subln_g[l]), lam_init, at_d, ctx.shape[1])

    u3 = _dwconv(z_hy, hy_conv_w[l], row2(hy_conv_b[l]))
    zemb, decay = _filter_tables(S, emb, width)
    epad = (-emb) % V7X_LANES
    zemb = jnp.pad(zemb, ((0, 0), (0, epad)))
    w1p = jnp.pad(hy_f_w1[l], ((0, epad), (0, 0)))
    f_taps, b_taps = _filt(zemb, w1p, row2(hy_f_b1[l]), hy_f_w2[l], row2(hy_f_b2[l]), hy_f_w3[l],
                           row2(hy_f_b3[l]), hy_f_freq[l], hy_f_wout[l], decay, width, order)
    a_len, kp, fa, fai, gc, gci = _fft_tables(S)
    kr, ki = _fspec(f_taps, b_taps, fa, gc, a_len, kp)
    zz = _lconv(u3, 0, u3, 1, fa, fai, gc, gci, kr, ki, row2(hy_bias[l, 0]), 0, width, a_len, kp)
    y_hy = _lconv(zz, 0, u3, 2, fa, fai, gc, gci, kr, ki, row2(hy_bias[l, 1]), 1, width, a_len, kp)

    x1 = _merge(x, y_hy, o_at, gates, g1, lng, lnb, row2(ln1_g[l]), row2(ln1_b[l]),
                w_hy_o[l].astype(BF16), w_at_o[l].astype(BF16), w_out[l].astype(BF16), alpha)
    return _ffn(x1, sh2, sc2, g2, ffn_w_up[l].astype(BF16), ffn_conv_w[l], row2(ffn_conv_b[l]),
                ffn_w_down[l].astype(BF16), row2(ln2_g[l]), row2(ln2_b[l]), alpha)
```

```python
import functools
import math

import numpy as np
import jax
import jax.numpy as jnp
from jax import lax
from jax.experimental import pallas as pl
from jax.experimental.pallas import tpu as pltpu

F32 = jnp.float32
BF16 = jnp.bfloat16

LN_EPS = 1e-5
RMS_EPS = 1e-5
GRID_W = 64
ROPE_BASE = 10000.0
N_ADA = 6
HY_DECAY_TARGET = 1e-2
HY_FAST_PCT = 0.3
HY_SLOW_PCT = 1.5
CONV_W = 3

V7X_LANES = 128
V7X_SUBLANES = 8
V7X_VMEM_BYTES = 64 * 1024 * 1024
VMEM_LIMIT = 56 * 1024 * 1024

HIGHEST = lax.Precision.HIGHEST


def _cparams(n_axes, vmem=None):
    return pltpu.CompilerParams(dimension_semantics=("arbitrary",) * n_axes, vmem_limit_bytes=vmem)


def _layer_norm(x, g, b):
    mu = jnp.mean(x, axis=-1, keepdims=True)
    xc = x - mu
    var = jnp.mean(xc * xc, axis=-1, keepdims=True)
    return xc * lax.rsqrt(var + LN_EPS) * g + b


def _sigmoid(x):
    return 1.0 / (1.0 + jnp.exp(-x))


def _dot(a, b):
    return jnp.dot(a, b, preferred_element_type=F32)


def _ada_kernel(c_ref, w_ref, b_ref, o_ref):
    c = c_ref[...]
    s = c * _sigmoid(c)
    o_ref[...] = jnp.dot(s, w_ref[...], preferred_element_type=F32, precision=HIGHEST) + b_ref[...]


def _ada(cc, w, b, tn=1536):
    rows, d = cc.shape
    n = w.shape[1]
    return pl.pallas_call(
        _ada_kernel,
        grid=(n // tn,),
        in_specs=[pl.BlockSpec((rows, d), lambda j: (0, 0)),
                  pl.BlockSpec((d, tn), lambda j: (0, j)),
                  pl.BlockSpec((1, tn), lambda j: (0, j))],
        out_specs=pl.BlockSpec((rows, tn), lambda j: (0, j)),
        out_shape=jax.ShapeDtypeStruct((rows, n), F32),
        compiler_params=_cparams(1, VMEM_LIMIT),
        name="ada",
    )(cc, w, b)


def _rope(x, cos, sin_a, sin_b, n_heads):
    outs = []
    for h in range(n_heads):
        xh = x[:, h * V7X_LANES:(h + 1) * V7X_LANES]
        up = pltpu.roll(xh, V7X_LANES - 16, 1)
        dn = pltpu.roll(xh, 16, 1)
        outs.append(xh * cos + up * sin_a + dn * sin_b)
    return jnp.concatenate(outs, axis=1)


def _inproj_kernel(x_ref, sh_ref, sc_ref, lng_ref, lnb_ref, w_ref, cos_ref, sa_ref, sb_ref,
                   hy_ref, q_ref, k_ref, v_ref, g_ref, *, hy_cols, at_w, q_scale):
    xn = _layer_norm(x_ref[0], lng_ref[...], lnb_ref[...])
    h = (xn * (1.0 + sc_ref[0]) + sh_ref[0]).astype(BF16)
    n_heads = at_w // V7X_LANES
    o1 = hy_cols
    o2, o3, o4 = o1 + at_w, o1 + 2 * at_w, o1 + 3 * at_w
    hy_ref[0] = _dot(h, w_ref[:, 0:o1]).astype(hy_ref.dtype)
    cos, sa, sb = cos_ref[...], sa_ref[...], sb_ref[...]
    q = _rope(_dot(h, w_ref[:, o1:o2]), cos, sa, sb, n_heads)
    q_ref[0] = (q * q_scale).T.astype(q_ref.dtype)
    k = _rope(_dot(h, w_ref[:, o2:o3]), cos, sa, sb, n_heads)
    k_ref[0] = k.astype(k_ref.dtype)
    v_ref[0] = _dot(h, w_ref[:, o3:o4]).T.astype(v_ref.dtype)
    g_ref[0] = _dot(h, w_ref[:, o4:]).astype(g_ref.dtype)


def _inproj(x, sh, sc, lng, lnb, w, cos, sa, sb, hy_cols, at_w, at_d, n_ctx, tm=512):
    B, S, D = x.shape
    Lk = S + n_ctx
    ncols = w.shape[1]
    g_cols = ncols - hy_cols - 3 * at_w
    kern = functools.partial(_inproj_kernel, hy_cols=hy_cols, at_w=at_w, q_scale=at_d ** -0.5 * math.log2(math.e))
    row = lambda i, b: (b, i, 0)
    rowt = lambda i, b: (b, 0, i)
    mod = lambda i, b: (b, 0, 0)
    const = lambda i, b: (0, 0)
    tab = lambda i, b: (i, 0)
    return pl.pallas_call(
        kern,
        grid=(S // tm, B),
        in_specs=[pl.BlockSpec((1, tm, D), row),
                  pl.BlockSpec((1, 1, D), mod), pl.BlockSpec((1, 1, D), mod),
                  pl.BlockSpec((1, D), const), pl.BlockSpec((1, D), const),
                  pl.BlockSpec((D, ncols), const, pipeline_mode=pl.Buffered(1)),
                  pl.BlockSpec((tm, V7X_LANES), tab), pl.BlockSpec((tm, V7X_LANES), tab),
                  pl.BlockSpec((tm, V7X_LANES), tab)],
        out_specs=[pl.BlockSpec((1, tm, hy_cols), row), pl.BlockSpec((1, at_w, tm), rowt),
                   pl.BlockSpec((1, tm, at_w), row), pl.BlockSpec((1, at_w, tm), rowt),
                   pl.BlockSpec((1, tm, g_cols), row)],
        out_shape=[jax.ShapeDtypeStruct((B, S, hy_cols), BF16), jax.ShapeDtypeStruct((B, at_w, S), BF16),
                   jax.ShapeDtypeStruct((B, Lk, at_w), BF16), jax.ShapeDtypeStruct((B, at_w, Lk), BF16),
                   jax.ShapeDtypeStruct((B, S, g_cols), BF16)],
        compiler_params=_cparams(2, VMEM_LIMIT),
        name="inproj",
    )(x, sh, sc, lng, lnb, w, cos, sa, sb)


def _ctx_kv_kernel(x_ref, sh_ref, sc_ref, lng_ref, lnb_ref, w_ref, k_in_ref, v_in_ref, k_ref, v_ref, *, at_w):
    del k_in_ref, v_in_ref
    xn = _layer_norm(x_ref[0], lng_ref[...], lnb_ref[...])
    h = (xn * (1.0 + sc_ref[...]) + sh_ref[...]).astype(BF16)
    k_ref[0] = _dot(h, w_ref[:, 0:at_w]).astype(k_ref.dtype)
    v_ref[0] = _dot(h, w_ref[:, at_w:]).T.astype(v_ref.dtype)


def _ctx_kv(ctx, sh, sc, lng, lnb, w_kv, k_buf, vt_buf, at_w):
    B, C, D = ctx.shape
    S = k_buf.shape[1] - C
    assert S % C == 0
    const = lambda b: (0, 0)
    row = lambda b: (b, 0, 0)
    return pl.pallas_call(
        functools.partial(_ctx_kv_kernel, at_w=at_w),
        grid=(B,),
        in_specs=[pl.BlockSpec((1, C, D), row),
                  pl.BlockSpec((1, D), const), pl.BlockSpec((1, D), const),
                  pl.BlockSpec((1, D), const), pl.BlockSpec((1, D), const),
                  pl.BlockSpec((D, 2 * at_w), const),
                  pl.BlockSpec(memory_space=pl.ANY), pl.BlockSpec(memory_space=pl.ANY)],
        out_specs=[pl.BlockSpec((1, C, at_w), lambda b: (b, S // C, 0)),
                   pl.BlockSpec((1, at_w, C), lambda b: (b, 0, S // C))],
        out_shape=[jax.ShapeDtypeStruct(k_buf.shape, BF16), jax.ShapeDtypeStruct(vt_buf.shape, BF16)],
        input_output_aliases={6: 0, 7: 1},
        compiler_params=_cparams(1),
        name="ctx_kv",
    )(ctx, sh, sc, lng, lnb, w_kv, k_buf, vt_buf)


ATTN_AHEAD = 2


def _attn_kernel(qt_ref, k_ref, vt_ref, lq1_ref, lk1_ref, lq2_ref, lk2_ref, g_ref, o_ref, *,
                 lam_init, at_d, chunks):
    qt = qt_ref[0]
    hw, tq = qt.shape
    lam = (jnp.exp(jnp.sum(lq1_ref[...] * lk1_ref[...], axis=-1, keepdims=True))
           - jnp.exp(jnp.sum(lq2_ref[...] * lk2_ref[...], axis=-1, keepdims=True)) + lam_init)
    row = lax.broadcasted_iota(jnp.int32, qt.shape, 0)
    zero = jnp.zeros_like(qt)
    qts = (jnp.where(row < at_d, qt, zero), jnp.where(row >= at_d, qt, zero))
    state = [(jnp.full((1, tq), -1e30, F32), jnp.zeros((1, tq), F32), jnp.zeros((hw, tq), F32))
             for _ in range(2)]
    units = [(off, kc, comp) for off, kc in chunks for comp in range(2)]

    def scores(unit):
        off, kc, comp = unit
        return _dot(k_ref[0, off:off + kc, :], qts[comp])

    pending = [scores(u) for u in units[:ATTN_AHEAD]]
    for i, (off, kc, comp) in enumerate(units):
        if i + ATTN_AHEAD < len(units):
            pending.append(scores(units[i + ATTN_AHEAD]))
        st = pending.pop(0)
        m, l, acc = state[comp]
        m_new = jnp.maximum(m, jnp.max(st, axis=0, keepdims=True))
        alpha = jnp.exp2(m - m_new)
        p = jnp.exp2(st - m_new)
        l = alpha * l + jnp.sum(p, axis=0, keepdims=True)
        acc = alpha * acc + _dot(vt_ref[0, :, off:off + kc], p.astype(BF16))
        state[comp] = (m_new, l, acc)
    (_, l1, a1), (_, l2, a2) = state
    ot = a1 * (1.0 / l1) - a2 * (lam / l2)
    ot = ot * lax.rsqrt(jnp.mean(ot * ot, axis=0, keepdims=True) + RMS_EPS)
    o_ref[0] = (ot.T * g_ref[...] * (1.0 - lam_init)).astype(o_ref.dtype)


def _attn(qt, k_all, vt_all, lq1, lk1, lq2, lk2, subln_g, lam_init, at_d, n_ctx, tq=512, kc=1024):
    B, W, S = qt.shape
    Lk = k_all.shape[1]
    hw = 2 * at_d
    n_heads = W // hw
    assert (Lk - n_ctx) % kc == 0 and n_ctx % V7X_LANES == 0
    chunks = tuple((i * kc, kc) for i in range((Lk - n_ctx) // kc)) + ((Lk - n_ctx, n_ctx),)
    qmap = lambda b, h, i: (b, h, i)
    omap = lambda b, h, i: (b, i, h)
    const = lambda b, h, i: (0, 0)
    return pl.pallas_call(
        functools.partial(_attn_kernel, lam_init=lam_init, at_d=at_d, chunks=chunks),
        grid=(B, n_heads, S // tq),
        in_specs=[pl.BlockSpec((1, hw, tq), qmap),
                  pl.BlockSpec((1, Lk, hw), lambda b, h, i: (b, 0, h)),
                  pl.BlockSpec((1, hw, Lk), lambda b, h, i: (b, h, 0)),
                  pl.BlockSpec((1, at_d), const), pl.BlockSpec((1, at_d), const),
                  pl.BlockSpec((1, at_d), const), pl.BlockSpec((1, at_d), const),
                  pl.BlockSpec((1, hw), const)],
        out_specs=pl.BlockSpec((1, tq, hw), omap),
        out_shape=jax.ShapeDtypeStruct((B, S, W), BF16),
        compiler_params=_cparams(3, VMEM_LIMIT),
        name="attn",
    )(qt, k_all, vt_all, lq1, lk1, lq2, lk2, subln_g)


def _dwconv_kernel(z_ref, w_ref, b_ref, o_ref):
    z = z_ref[0].astype(F32)
    L = z.shape[0]
    row = lax.broadcasted_iota(jnp.int32, z.shape, 0)
    zm = jnp.where(row == 0, 0.0, pltpu.roll(z, 1, 0))
    zp = jnp.where(row == L - 1, 0.0, pltpu.roll(z, L - 1, 0))
    o_ref[0, 0] = (zm * w_ref[0:1, :] + z * w_ref[1:2, :] + zp * w_ref[2:3, :] + b_ref[...]).astype(o_ref.dtype)


def _dwconv(z, w, b, tc):
    B, L, C = z.shape
    return pl.pallas_call(
        _dwconv_kernel,
        grid=(B, C // tc),
        in_specs=[pl.BlockSpec((1, L, tc), lambda b, j: (b, 0, j)),
                  pl.BlockSpec((CONV_W, tc), lambda b, j: (0, j)),
                  pl.BlockSpec((1, tc), lambda b, j: (0, j))],
        out_specs=pl.BlockSpec((1, 1, L, tc), lambda b, j: (b, j, 0, 0)),
        out_shape=jax.ShapeDtypeStruct((B, C // tc, L, tc), BF16),
        compiler_params=_cparams(2, VMEM_LIMIT),
        name="dwconv",
    )(z, w, b)


def _filt_kernel(z_ref, w1_ref, b1_ref, w2_ref, b2_ref, w3_ref, b3_ref, fr_ref, wo_ref, dec_ref,
                 f_ref, b_ref, *, width, order):
    i = pl.program_id(0)
    tc = f_ref.shape[2]
    hdot = lambda a, b: jnp.dot(a, b, preferred_element_type=F32, precision=HIGHEST)
    hdn = jnp.sin(fr_ref[0:1, :] * (hdot(z_ref[...], w1_ref[...]) + b1_ref[...]))
    hdn = jnp.sin(fr_ref[1:2, :] * (hdot(hdn, w2_ref[...]) + b2_ref[...]))
    hdn = jnp.sin(fr_ref[2:3, :] * (hdot(hdn, w3_ref[...]) + b3_ref[...]))
    h = hdot(hdn, wo_ref[...])
    dec = dec_ref[...]
    tl = dec.shape[0]
    row = lax.broadcasted_iota(jnp.int32, (tl, width), 0) + i * tl
    for o in range(order):
        fwd = h[:, (2 * o) * width:(2 * o + 1) * width] * dec
        bwd = h[:, (2 * o + 1) * width:(2 * o + 2) * width] * dec
        bwd = jnp.where(row == 0, 0.0, bwd)
        for q in range(width // tc):
            f_ref[o * (width // tc) + q] = fwd[:, q * tc:(q + 1) * tc]
            b_ref[o * (width // tc) + q] = bwd[:, q * tc:(q + 1) * tc]


def _filt(zemb, w1, b1, w2, b2, w3, b3, freq, wout, decay, width, order, tc, tl=512):
    L, E = zemb.shape
    F = w2.shape[0]
    nq = order * width // tc
    const = lambda i: (0, 0)
    rowm = lambda i: (i, 0)
    blk = lambda i: (0, i, 0)
    return pl.pallas_call(
        functools.partial(_filt_kernel, width=width, order=order),
        grid=(L // tl,),
        in_specs=[pl.BlockSpec((tl, E), rowm),
                  pl.BlockSpec((E, F), const), pl.BlockSpec((1, F), const),
                  pl.BlockSpec((F, F), const), pl.BlockSpec((1, F), const),
                  pl.BlockSpec((F, F), const), pl.BlockSpec((1, F), const),
                  pl.BlockSpec((3, F), const),
                  pl.BlockSpec((F, order * 2 * width), const),
                  pl.BlockSpec((tl, width), rowm)],
        out_specs=[pl.BlockSpec((nq, tl, tc), blk), pl.BlockSpec((nq, tl, tc), blk)],
        out_shape=[jax.ShapeDtypeStruct((nq, L, tc), F32), jax.ShapeDtypeStruct((nq, L, tc), F32)],
        compiler_params=_cparams(1, VMEM_LIMIT),
        name="filt",
    )(zemb, w1, b1, w2, b2, w3, b3, freq, wout, decay)


FFT_NB = 64
FFT_GROUP = 8
FFT_TC = 256


def _ld(ref, start, size, stride=None):
    idx = pl.ds(start, size) if stride is None else pl.ds(start, size, stride=stride)
    return jnp.concatenate([ref[t, idx, :] for t in range(ref.shape[0])], axis=1)


def _st(ref, start, size, val, stride=None):
    idx = pl.ds(start, size) if stride is None else pl.ds(start, size, stride=stride)
    for t in range(ref.shape[0]):
        ref[t, idx, :] = val[:, t * V7X_LANES:(t + 1) * V7X_LANES]


def _residue_lanes(r, tc):
    return pl.ds(pl.multiple_of(r * tc, V7X_LANES), tc)


def _fft_stage_a(load_residue, s_ref, fa, *, kp):
    def body(g, carry):
        for j in range(FFT_GROUP):
            r = g * FFT_GROUP + j
            ur = load_residue(r).astype(BF16)
            _st(s_ref, pl.multiple_of(r * (2 * kp), V7X_SUBLANES), 2 * kp, _dot(fa, ur))
        return carry
    lax.fori_loop(0, FFT_NB // FFT_GROUP, body, 0)


def _fft_stage_c(s_ref, gc_ref, k1, kp):
    sr = _ld(s_ref, k1, FFT_NB, 2 * kp)
    si = _ld(s_ref, kp + k1, FFT_NB, 2 * kp)
    x = _dot(gc_ref[k1], jnp.concatenate([sr, si], axis=0).astype(BF16))
    return x[:FFT_NB], x[FFT_NB:]


def _fspec_kernel(f_ref, b_ref, fa_ref, gc_ref, kr_ref, ki_ref, s_ref, *, kp):
    fa = fa_ref[...]
    tc = kr_ref.shape[2]

    def transform(src_ref, emit):
        _fft_stage_a(lambda r: src_ref[0, :, _residue_lanes(r, tc)], s_ref, fa, kp=kp)

        def body(g, carry):
            for j in range(FFT_GROUP):
                k1 = g * FFT_GROUP + j
                xr, xi = _fft_stage_c(s_ref, gc_ref, k1, kp)
                emit(k1, xr, xi)
            return carry
        lax.fori_loop(0, kp // FFT_GROUP, body, 0)

    def emit_fwd(k1, xr, xi):
        kr_ref[k1] = xr
        ki_ref[k1] = xi

    def emit_bwd(k1, xr, xi):
        kr_ref[k1] += xr
        ki_ref[k1] -= xi

    transform(f_ref, emit_fwd)
    transform(b_ref, emit_bwd)


def _fspec(f_taps, b_taps, fa, gc, a_len, kp):
    nq, L, tc = f_taps.shape
    view = lambda t: t.reshape(nq, a_len, FFT_NB * tc)
    const2 = lambda j: (0, 0)
    const3 = lambda j: (0, 0, 0)
    blk = lambda j: (j, 0, 0)
    return pl.pallas_call(
        functools.partial(_fspec_kernel, kp=kp),
        grid=(nq,),
        in_specs=[pl.BlockSpec((1, a_len, FFT_NB * tc), blk), pl.BlockSpec((1, a_len, FFT_NB * tc), blk),
                  pl.BlockSpec(fa.shape, const2), pl.BlockSpec(gc.shape, const3)],
        out_specs=[pl.BlockSpec((kp, FFT_NB, tc), lambda j: (0, 0, j)),
                   pl.BlockSpec((kp, FFT_NB, tc), lambda j: (0, 0, j))],
        out_shape=[jax.ShapeDtypeStruct((kp, FFT_NB, nq * tc), F32), jax.ShapeDtypeStruct((kp, FFT_NB, nq * tc), F32)],
        scratch_shapes=[pltpu.VMEM((tc // V7X_LANES, FFT_NB * 2 * kp, V7X_LANES), F32)],
        compiler_params=_cparams(1, VMEM_LIMIT),
        name="fspec",
    )(view(f_taps), view(b_taps), fa, gc)


def _lconv_kernel(u_ref, gate_ref, fa_ref, fai_ref, gc_ref, gci_ref, kr_ref, ki_ref, bias_ref, o_ref,
                  s_ref, *, kp):
    tc = bias_ref.shape[1]
    _fft_stage_a(lambda r: u_ref[0, 0, :, _residue_lanes(r, tc)], s_ref, fa_ref[...], kp=kp)

    def spectrum_product(g, carry):
        k1s = [g * FFT_GROUP + j for j in range(FFT_GROUP)]
        xs = [_fft_stage_c(s_ref, gc_ref, k1, kp) for k1 in k1s]
        outs = []
        for k1, (xr, xi) in zip(k1s, xs):
            kr, ki = kr_ref[k1], ki_ref[k1]
            y = jnp.concatenate([xr * kr - xi * ki, xr * ki + xi * kr], axis=0).astype(BF16)
            outs.append(_dot(gci_ref[k1], y))
        for k1, bc in zip(k1s, outs):
            _st(s_ref, k1, FFT_NB, bc[:FFT_NB], 2 * kp)
            _st(s_ref, kp + k1, FFT_NB, bc[FFT_NB:], 2 * kp)
        return carry
    lax.fori_loop(0, kp // FFT_GROUP, spectrum_product, 0)

    fai = fai_ref[...]
    bias = bias_ref[...]

    def inverse_a(g, carry):
        for j in range(FFT_GROUP):
            r = g * FFT_GROUP + j
            lanes = _residue_lanes(r, tc)
            slab = _ld(s_ref, pl.multiple_of(r * (2 * kp), V7X_SUBLANES), 2 * kp).astype(BF16)
            y = _dot(fai, slab)
            uf = u_ref[0, 0, :, lanes].astype(F32)
            gate = gate_ref[0, 0, :, lanes].astype(F32)
            o_ref[0, 0, :, lanes] = (gate * (y + uf * bias)).astype(o_ref.dtype)
        return carry
    lax.fori_loop(0, FFT_NB // FFT_GROUP, inverse_a, 0)


def _lconv(u_arr, u_blk, gate_arr, gate_blk, fa, fai, gc, gci, kr, ki, bias, order_idx, a_len, kp):
    B, _, L, tc = u_arr.shape
    nh = bias.shape[1] // tc
    view = lambda t: t.reshape(t.shape[0], t.shape[1], a_len, FFT_NB * tc)
    const2 = lambda h, b: (0, 0)
    const3 = lambda h, b: (0, 0, 0)
    spec = lambda h, b: (0, 0, order_idx * nh + h)
    sig = (1, 1, a_len, FFT_NB * tc)
    out = pl.pallas_call(
        functools.partial(_lconv_kernel, kp=kp),
        grid=(nh, B),
        in_specs=[pl.BlockSpec(sig, lambda h, b: (b, u_blk + h, 0, 0)),
                  pl.BlockSpec(sig, lambda h, b: (b, gate_blk + h, 0, 0)),
                  pl.BlockSpec(fa.shape, const2), pl.BlockSpec(fai.shape, const2),
                  pl.BlockSpec(gc.shape, const3, pipeline_mode=pl.Buffered(1)),
                  pl.BlockSpec(gci.shape, const3, pipeline_mode=pl.Buffered(1)),
                  pl.BlockSpec((kp, FFT_NB, tc), spec, pipeline_mode=pl.Buffered(1)),
                  pl.BlockSpec((kp, FFT_NB, tc), spec, pipeline_mode=pl.Buffered(1)),
                  pl.BlockSpec((1, tc), lambda h, b: (0, h))],
        out_specs=pl.BlockSpec(sig, lambda h, b: (b, h, 0, 0)),
        out_shape=jax.ShapeDtypeStruct((B, nh, a_len, FFT_NB * tc), BF16),
        scratch_shapes=[pltpu.VMEM((tc // V7X_LANES, FFT_NB * 2 * kp, V7X_LANES), F32)],
        compiler_params=_cparams(2, VMEM_LIMIT),
        name="lconv%d" % order_idx,
    )(view(u_arr), view(gate_arr), fa, fai, gc, gci, kr, ki, bias)
    return out.reshape(B, nh, L, tc)


def _merge_kernel(x_ref, yhy_ref, oat_ref, g_ref, g1_ref, lng_ref, lnb_ref, l1g_ref, l1b_ref,
                  whyo_ref, wato_ref, wout_ref, o_ref, *, alpha):
    d = whyo_ref.shape[1]
    xln = _layer_norm(x_ref[0], lng_ref[...], lnb_ref[...])
    g = g_ref[0].astype(F32)
    tc = yhy_ref.shape[3]
    y_proj = sum(_dot(yhy_ref[0, q], whyo_ref[q * tc:(q + 1) * tc, :]) for q in range(yhy_ref.shape[1]))
    m = (_sigmoid(g[:, :d]) * y_proj
         + _sigmoid(g[:, d:]) * _dot(oat_ref[0], wato_ref[...]))
    y = _dot(m.astype(BF16), wout_ref[...])
    o_ref[0] = _layer_norm(alpha * xln + g1_ref[0] * y, l1g_ref[...], l1b_ref[...])


def _merge(x, yhy, oat, g, g1, lng, lnb, l1g, l1b, whyo, wato, wout, alpha, tm=512):
    B, S, D = x.shape
    row = lambda b, i: (b, i, 0)
    mod = lambda b, i: (b, 0, 0)
    const = lambda b, i: (0, 0)
    vec = pl.BlockSpec((1, D), const)
    return pl.pallas_call(
        functools.partial(_merge_kernel, alpha=alpha),
        grid=(B, S // tm),
        in_specs=[pl.BlockSpec((1, tm, D), row),
                  pl.BlockSpec((1, yhy.shape[1], tm, yhy.shape[3]), lambda b, i: (b, 0, i, 0)),
                  pl.BlockSpec((1, tm, oat.shape[2]), row),
                  pl.BlockSpec((1, tm, g.shape[2]), row),
                  pl.BlockSpec((1, 1, D), mod), vec, vec, vec, vec,
                  pl.BlockSpec(whyo.shape, const), pl.BlockSpec(wato.shape, const), pl.BlockSpec(wout.shape, const)],
        out_specs=pl.BlockSpec((1, tm, D), row),
        out_shape=jax.ShapeDtypeStruct((B, S, D), F32),
        compiler_params=_cparams(2, VMEM_LIMIT),
        name="merge",
    )(x, yhy, oat, g, g1, lng, lnb, l1g, l1b, whyo, wato, wout)


def _ffn_kernel(x_ref, xp_ref, xn_ref, sh_ref, sc_ref, g2_ref, wup_ref, cw_ref, cb_ref, wdn_ref,
                l2g_ref, l2b_ref, o_ref, *, alpha, d_ff, tc, halo):
    i = pl.program_id(1)
    n_i = pl.num_programs(1)
    x = x_ref[0]
    tm = x.shape[0]
    sc = 1.0 + sc_ref[0]
    sh = sh_ref[0]
    hp = jnp.where(i > 0, xp_ref[0] * sc + sh, 0.0)
    hn = jnp.where(i < n_i - 1, xn_ref[0] * sc + sh, 0.0)
    h = jnp.concatenate([hp, x * sc + sh, hn], axis=0).astype(BF16)
    rows = tm + 2 * halo

    def conv(u, lo):
        w = cw_ref[:, lo:lo + tc]
        um = pltpu.roll(u, 1, 0)[halo:halo + tm]
        up = pltpu.roll(u, rows - 1, 0)[halo:halo + tm]
        return um * w[0:1] + u[halo:halo + tm] * w[1:2] + up * w[2:3] + cb_ref[:, lo:lo + tc]

    def up(lo):
        return _dot(h, wup_ref[:, lo:lo + tc]), _dot(h, wup_ref[:, d_ff + lo:d_ff + lo + tc])

    n_chunks = d_ff // tc
    acc = jnp.zeros((tm, o_ref.shape[2]), F32)
    cur = up(0)
    for c in range(n_chunks):
        nxt = up((c + 1) * tc) if c + 1 < n_chunks else None
        a = conv(cur[0], c * tc)
        g = conv(cur[1], d_ff + c * tc)
        act = (g * _sigmoid(g) * a).astype(BF16)
        acc = acc + _dot(act, wdn_ref[c * tc:(c + 1) * tc, :])
        cur = nxt
    o_ref[0] = _layer_norm(alpha * x + g2_ref[0] * acc, l2g_ref[...], l2b_ref[...])


def _ffn(x1, sh, sc, g2, wup, cw, cb, wdn, l2g, l2b, alpha, tm=512, tc=256):
    B, S, D = x1.shape
    d_ff = wdn.shape[0]
    halo = V7X_SUBLANES
    nb = tm // halo
    row = lambda b, i: (b, i, 0)
    prev = lambda b, i: (b, jnp.maximum(i * nb - 1, 0), 0)
    nxt = lambda b, i: (b, jnp.minimum((i + 1) * nb, S // halo - 1), 0)
    mod = lambda b, i: (b, 0, 0)
    const = lambda b, i: (0, 0)
    vec = pl.BlockSpec((1, D), const)
    return pl.pallas_call(
        functools.partial(_ffn_kernel, alpha=alpha, d_ff=d_ff, tc=tc, halo=halo),
        grid=(B, S // tm),
        in_specs=[pl.BlockSpec((1, tm, D), row), pl.BlockSpec((1, halo, D), prev), pl.BlockSpec((1, halo, D), nxt),
                  pl.BlockSpec((1, 1, D), mod), pl.BlockSpec((1, 1, D), mod), pl.BlockSpec((1, 1, D), mod),
                  pl.BlockSpec(wup.shape, const, pipeline_mode=pl.Buffered(1)),
                  pl.BlockSpec(cw.shape, const), pl.BlockSpec(cb.shape, const),
                  pl.BlockSpec(wdn.shape, const, pipeline_mode=pl.Buffered(1)),
                  vec, vec],
        out_specs=pl.BlockSpec((1, tm, D), row),
        out_shape=jax.ShapeDtypeStruct((B, S, D), F32),
        compiler_params=_cparams(2, VMEM_LIMIT),
        name="ffn",
    )(x1, x1, x1, sh, sc, g2, wup, cw, cb, wdn, l2g, l2b)


def _rope_tables(S, at_d):
    rope_axis = at_d // 2
    half = rope_axis // 2
    pos = jnp.arange(S, dtype=jnp.int32)
    rowp = (pos // GRID_W).astype(F32)
    colp = (pos % GRID_W).astype(F32)
    inv = ROPE_BASE ** (-jnp.arange(0, rope_axis, 2, dtype=F32) / rope_axis)
    lane = jnp.arange(V7X_LANES)
    d = lane % at_d
    e = d % rope_axis
    f = e % half
    ang = jnp.where((d < rope_axis)[None, :], rowp[:, None], colp[:, None]) * inv[f][None, :]
    first = (e < half)[None, :]
    sin = jnp.sin(ang)
    return jnp.cos(ang), jnp.where(first, -sin, 0.0), jnp.where(first, 0.0, sin)


def _filter_tables(L, emb, width):
    bands = (emb - 1) // 2
    t = jnp.linspace(0.0, 1.0, L, dtype=F32)[:, None]
    w = (2.0 * math.pi / L) * jnp.arange(L, dtype=F32)[:, None]
    f = jnp.linspace(1e-4, bands - 1, bands, dtype=F32)[None, :]
    z = jnp.concatenate([t, jnp.cos(f * w), -jnp.sin(f * w)], -1)
    min_decay = math.log(HY_DECAY_TARGET) / HY_SLOW_PCT
    max_decay = math.log(HY_DECAY_TARGET) / HY_FAST_PCT
    deltas = jnp.linspace(min_decay, max_decay, width, dtype=F32)
    decay = jnp.exp(-t * jnp.abs(deltas)[None, :])
    return z, decay


def _fft_tables(L):
    a_len = L // FFT_NB
    assert a_len * FFT_NB == L and a_len % FFT_GROUP == 0
    n_fft, n1 = 2 * L, 2 * a_len
    kp = -(-(a_len + 1) // FFT_GROUP) * FFT_GROUP
    k1 = np.arange(kp)
    valid = (k1 <= a_len).astype(np.float64)
    ph = (np.outer(k1, np.arange(a_len)) % n1) * (2.0 * np.pi / n1)
    ca, sa = np.cos(ph) * valid[:, None], np.sin(ph) * valid[:, None]
    fa = np.concatenate([ca, -sa], axis=0)
    w = np.where((k1 == 0) | (k1 == a_len), 1.0, 2.0) / n_fft
    fai = np.concatenate([(ca * w[:, None]).T, (-sa * w[:, None]).T], axis=1)
    r = np.arange(FFT_NB)
    kk = k1[:, None, None] + n1 * r[None, :, None]
    th = ((kk * r[None, None, :]) % n_fft) * (2.0 * np.pi / n_fft)
    gr, gi = np.cos(th) * valid[:, None, None], -np.sin(th) * valid[:, None, None]
    gc = np.concatenate([np.concatenate([gr, -gi], 2), np.concatenate([gi, gr], 2)], 1)
    grt, git = gr.transpose(0, 2, 1), gi.transpose(0, 2, 1)
    gci = np.concatenate([np.concatenate([grt, git], 2), np.concatenate([-git, grt], 2)], 1)
    as_bf16 = lambda m: jnp.asarray(m, dtype=F32).astype(BF16)
    return a_len, kp, as_bf16(fa), as_bf16(fai), as_bf16(gc), as_bf16(gci)


def kernel(x, c, ctx, c_ctx, ln_in_g, ln_in_b, w_ada, b_ada, w_in, hy_conv_w, hy_conv_b, hy_f_w1, hy_f_b1, hy_f_w2, hy_f_b2, hy_f_w3, hy_f_b3, hy_f_freq, hy_f_wout, hy_bias, lam_q1, lam_k1, lam_q2, lam_k2, at_subln_g, w_hy_o, w_at_o, w_out, ln1_g, ln1_b, ffn_w_up, ffn_conv_w, ffn_conv_b, ffn_w_down, ln2_g, ln2_b):
    B, S, D = x.shape
    depth = w_ada.shape[0]
    assert depth == 1, "single-layer configuration only"
    l = 0
    order, width = hy_bias.shape[1], hy_bias.shape[2]
    at_d = lam_q1.shape[1]
    at_w = w_at_o.shape[1]
    hy_cols = (order + 1) * width
    emb = hy_f_w1.shape[1]
    alpha = (2.0 * depth) ** 0.25
    lam_init = 0.8 - 0.6 * math.exp(-0.3 * l)
    assert 2 * at_d == V7X_LANES and order == 2

    row2 = lambda a: a.reshape(1, -1)

    pad = (-(B + 1)) % V7X_SUBLANES
    cc = jnp.concatenate([c, c_ctx[None, :], jnp.zeros((pad, D), F32)], 0)
    mod = _ada(cc, w_ada[l], row2(b_ada[l]))
    sh1, sc1, g1, sh2, sc2, g2 = [mod[:B, i * D:(i + 1) * D].reshape(B, 1, D) for i in range(N_ADA)]
    sh1c, sc1c = mod[B:B + 1, 0:D], mod[B:B + 1, D:2 * D]

    lng, lnb = row2(ln_in_g), row2(ln_in_b)
    w_in_b = w_in[l].astype(BF16)
    cos, sin_a, sin_b = _rope_tables(S, at_d)
    z_hy, qt, k_l, vt_l, gates = _inproj(x, sh1, sc1, lng, lnb, w_in_b, cos, sin_a, sin_b, hy_cols, at_w, at_d,
                                         ctx.shape[1])
    k_all, vt_all = _ctx_kv(ctx, sh1c, sc1c, lng, lnb, w_in_b[:, hy_cols + at_w:hy_cols + 3 * at_w],
                            k_l, vt_l, at_w)
    o_at = _attn(qt, k_all, vt_all, row2(lam_q1[l]), row2(lam_k1[l]), row2(lam_q2[l]), row2(lam_k2[l]),
                 row2(at_subln_g[l]), lam_init, at_d, ctx.shape[1])

    nh = width // FFT_TC
    u3 = _dwconv(z_hy, hy_conv_w[l], row2(hy_conv_b[l]), FFT_TC)
    zemb, decay = _filter_tables(S, emb, width)
    epad = (-emb) % V7X_LANES
    zemb = jnp.pad(zemb, ((0, 0), (0, epad)))
    w1p = jnp.pad(hy_f_w1[l], ((0, epad), (0, 0)))
    f_taps, b_taps = _filt(zemb, w1p, row2(hy_f_b1[l]), hy_f_w2[l], row2(hy_f_b2[l]), hy_f_w3[l],
                           row2(hy_f_b3[l]), hy_f_freq[l], hy_f_wout[l], decay, width, order, FFT_TC)
    a_len, kp, fa, fai, gc, gci = _fft_tables(S)
    kr, ki = _fspec(f_taps, b_taps, fa, gc, a_len, kp)
    zz = _lconv(u3, 0, u3, nh, fa, fai, gc, gci, kr, ki, row2(hy_bias[l, 0]), 0, a_len, kp)
    y_hy = _lconv(zz, 0, u3, 2 * nh, fa, fai, gc, gci, kr, ki, row2(hy_bias[l, 1]), 1, a_len, kp)

    x1 = _merge(x, y_hy, o_at, gates, g1, lng, lnb, row2(ln1_g[l]), row2(ln1_b[l]),
                w_hy_o[l].astype(BF16), w_at_o[l].astype(BF16), w_out[l].astype(BF16), alpha)
    return _ffn(x1, sh2, sc2, g2, ffn_w_up[l].astype(BF16), ffn_conv_w[l], row2(ffn_conv_b[l]),
                ffn_w_down[l].astype(BF16), row2(ln2_g[l]), row2(ln2_b[l]), alpha)
```

```python
import functools
import math

import numpy as np
import jax
import jax.numpy as jnp
from jax import lax
from jax.experimental import pallas as pl
from jax.experimental.pallas import tpu as pltpu

F32 = jnp.float32
BF16 = jnp.bfloat16

LN_EPS = 1e-5
RMS_EPS = 1e-5
GRID_W = 64
ROPE_BASE = 10000.0
N_ADA = 6
HY_DECAY_TARGET = 1e-2
HY_FAST_PCT = 0.3
HY_SLOW_PCT = 1.5
CONV_W = 3

V7X_LANES = 128
V7X_SUBLANES = 8
V7X_VMEM_BYTES = 64 * 1024 * 1024
VMEM_LIMIT = 56 * 1024 * 1024

HIGHEST = lax.Precision.HIGHEST


def _cparams(n_axes, vmem=None):
    return pltpu.CompilerParams(dimension_semantics=("arbitrary",) * n_axes, vmem_limit_bytes=vmem)


def _layer_norm(x, g, b):
    mu = jnp.mean(x, axis=-1, keepdims=True)
    xc = x - mu
    var = jnp.mean(xc * xc, axis=-1, keepdims=True)
    return xc * lax.rsqrt(var + LN_EPS) * g + b


def _sigmoid(x):
    return 1.0 / (1.0 + jnp.exp(-x))


def _dot(a, b):
    return jnp.dot(a, b, preferred_element_type=F32)


def _ada_kernel(c_ref, w_ref, b_ref, o_ref):
    c = c_ref[...]
    s = c * _sigmoid(c)
    o_ref[...] = jnp.dot(s, w_ref[...], preferred_element_type=F32, precision=HIGHEST) + b_ref[...]


def _ada(cc, w, b, tn=1536):
    rows, d = cc.shape
    n = w.shape[1]
    return pl.pallas_call(
        _ada_kernel,
        grid=(n // tn,),
        in_specs=[pl.BlockSpec((rows, d), lambda j: (0, 0)),
                  pl.BlockSpec((d, tn), lambda j: (0, j)),
                  pl.BlockSpec((1, tn), lambda j: (0, j))],
        out_specs=pl.BlockSpec((rows, tn), lambda j: (0, j)),
        out_shape=jax.ShapeDtypeStruct((rows, n), F32),
        compiler_params=_cparams(1, VMEM_LIMIT),
        name="ada",
    )(cc, w, b)


def _rope(x, cos, sin_a, sin_b, n_heads):
    outs = []
    for h in range(n_heads):
        xh = x[:, h * V7X_LANES:(h + 1) * V7X_LANES]
        up = pltpu.roll(xh, V7X_LANES - 16, 1)
        dn = pltpu.roll(xh, 16, 1)
        outs.append(xh * cos + up * sin_a + dn * sin_b)
    return jnp.concatenate(outs, axis=1)


def _inproj_kernel(x_ref, sh_ref, sc_ref, lng_ref, lnb_ref, w_ref, cos_ref, sa_ref, sb_ref,
                   hy_ref, q_ref, k_ref, v_ref, g_ref, *, hy_cols, at_w, q_scale):
    xn = _layer_norm(x_ref[0], lng_ref[...], lnb_ref[...])
    h = (xn * (1.0 + sc_ref[0]) + sh_ref[0]).astype(BF16)
    n_heads = at_w // V7X_LANES
    o1 = hy_cols
    o2, o3, o4 = o1 + at_w, o1 + 2 * at_w, o1 + 3 * at_w
    hy_ref[0] = _dot(h, w_ref[:, 0:o1]).astype(hy_ref.dtype)
    cos, sa, sb = cos_ref[...], sa_ref[...], sb_ref[...]
    q = _rope(_dot(h, w_ref[:, o1:o2]), cos, sa, sb, n_heads)
    q_ref[0] = (q * q_scale).T.astype(q_ref.dtype)
    k = _rope(_dot(h, w_ref[:, o2:o3]), cos, sa, sb, n_heads)
    k_ref[0] = k.astype(k_ref.dtype)
    v_ref[0] = _dot(h, w_ref[:, o3:o4]).T.astype(v_ref.dtype)
    g_ref[0] = _dot(h, w_ref[:, o4:]).astype(g_ref.dtype)


def _inproj(x, sh, sc, lng, lnb, w, cos, sa, sb, hy_cols, at_w, at_d, n_ctx, tm=512):
    B, S, D = x.shape
    Lk = S + n_ctx
    ncols = w.shape[1]
    g_cols = ncols - hy_cols - 3 * at_w
    kern = functools.partial(_inproj_kernel, hy_cols=hy_cols, at_w=at_w, q_scale=at_d ** -0.5 * math.log2(math.e))
    row = lambda i, b: (b, i, 0)
    rowt = lambda i, b: (b, 0, i)
    mod = lambda i, b: (b, 0, 0)
    const = lambda i, b: (0, 0)
    tab = lambda i, b: (i, 0)
    return pl.pallas_call(
        kern,
        grid=(S // tm, B),
        in_specs=[pl.BlockSpec((1, tm, D), row),
                  pl.BlockSpec((1, 1, D), mod), pl.BlockSpec((1, 1, D), mod),
                  pl.BlockSpec((1, D), const), pl.BlockSpec((1, D), const),
                  pl.BlockSpec((D, ncols), const, pipeline_mode=pl.Buffered(1)),
                  pl.BlockSpec((tm, V7X_LANES), tab), pl.BlockSpec((tm, V7X_LANES), tab),
                  pl.BlockSpec((tm, V7X_LANES), tab)],
        out_specs=[pl.BlockSpec((1, tm, hy_cols), row), pl.BlockSpec((1, at_w, tm), rowt),
                   pl.BlockSpec((1, tm, at_w), row), pl.BlockSpec((1, at_w, tm), rowt),
                   pl.BlockSpec((1, tm, g_cols), row)],
        out_shape=[jax.ShapeDtypeStruct((B, S, hy_cols), BF16), jax.ShapeDtypeStruct((B, at_w, S), BF16),
                   jax.ShapeDtypeStruct((B, Lk, at_w), BF16), jax.ShapeDtypeStruct((B, at_w, Lk), BF16),
                   jax.ShapeDtypeStruct((B, S, g_cols), BF16)],
        compiler_params=_cparams(2, VMEM_LIMIT),
        name="inproj",
    )(x, sh, sc, lng, lnb, w, cos, sa, sb)


def _ctx_kv_kernel(x_ref, sh_ref, sc_ref, lng_ref, lnb_ref, w_ref, k_in_ref, v_in_ref, k_ref, v_ref, *, at_w):
    del k_in_ref, v_in_ref
    xn = _layer_norm(x_ref[0], lng_ref[...], lnb_ref[...])
    h = (xn * (1.0 + sc_ref[...]) + sh_ref[...]).astype(BF16)
    k_ref[0] = _dot(h, w_ref[:, 0:at_w]).astype(k_ref.dtype)
    v_ref[0] = _dot(h, w_ref[:, at_w:]).T.astype(v_ref.dtype)


def _ctx_kv(ctx, sh, sc, lng, lnb, w_kv, k_buf, vt_buf, at_w):
    B, C, D = ctx.shape
    S = k_buf.shape[1] - C
    assert S % C == 0
    const = lambda b: (0, 0)
    row = lambda b: (b, 0, 0)
    return pl.pallas_call(
        functools.partial(_ctx_kv_kernel, at_w=at_w),
        grid=(B,),
        in_specs=[pl.BlockSpec((1, C, D), row),
                  pl.BlockSpec((1, D), const), pl.BlockSpec((1, D), const),
                  pl.BlockSpec((1, D), const), pl.BlockSpec((1, D), const),
                  pl.BlockSpec((D, 2 * at_w), const),
                  pl.BlockSpec(memory_space=pl.ANY), pl.BlockSpec(memory_space=pl.ANY)],
        out_specs=[pl.BlockSpec((1, C, at_w), lambda b: (b, S // C, 0)),
                   pl.BlockSpec((1, at_w, C), lambda b: (b, 0, S // C))],
        out_shape=[jax.ShapeDtypeStruct(k_buf.shape, BF16), jax.ShapeDtypeStruct(vt_buf.shape, BF16)],
        input_output_aliases={6: 0, 7: 1},
        compiler_params=_cparams(1),
        name="ctx_kv",
    )(ctx, sh, sc, lng, lnb, w_kv, k_buf, vt_buf)


ATTN_AHEAD = 2


def _attn_kernel(qt_ref, k_ref, vt_ref, lq1_ref, lk1_ref, lq2_ref, lk2_ref, g_ref, o_ref, *,
                 lam_init, at_d, chunks):
    qt = qt_ref[0]
    hw, tq = qt.shape
    lam = (jnp.exp(jnp.sum(lq1_ref[...] * lk1_ref[...], axis=-1, keepdims=True))
           - jnp.exp(jnp.sum(lq2_ref[...] * lk2_ref[...], axis=-1, keepdims=True)) + lam_init)
    row = lax.broadcasted_iota(jnp.int32, qt.shape, 0)
    zero = jnp.zeros_like(qt)
    qts = (jnp.where(row < at_d, qt, zero), jnp.where(row >= at_d, qt, zero))
    state = [(jnp.full((1, tq), -1e30, F32), jnp.zeros((1, tq), F32), jnp.zeros((hw, tq), F32))
             for _ in range(2)]
    units = [(off, kc, comp) for off, kc in chunks for comp in range(2)]

    def scores(unit):
        off, kc, comp = unit
        return _dot(k_ref[0, off:off + kc, :], qts[comp])

    pending = [scores(u) for u in units[:ATTN_AHEAD]]
    for i, (off, kc, comp) in enumerate(units):
        if i + ATTN_AHEAD < len(units):
            pending.append(scores(units[i + ATTN_AHEAD]))
        st = pending.pop(0)
        m, l, acc = state[comp]
        m_new = jnp.maximum(m, jnp.max(st, axis=0, keepdims=True))
        alpha = jnp.exp2(m - m_new)
        p = jnp.exp2(st - m_new)
        l = alpha * l + jnp.sum(p, axis=0, keepdims=True)
        acc = alpha * acc + _dot(vt_ref[0, :, off:off + kc], p.astype(BF16))
        state[comp] = (m_new, l, acc)
    (_, l1, a1), (_, l2, a2) = state
    ot = a1 * (1.0 / l1) - a2 * (lam / l2)
    ot = ot * lax.rsqrt(jnp.mean(ot * ot, axis=0, keepdims=True) + RMS_EPS)
    o_ref[0] = (ot.T * g_ref[...] * (1.0 - lam_init)).astype(o_ref.dtype)


def _attn(qt, k_all, vt_all, lq1, lk1, lq2, lk2, subln_g, lam_init, at_d, n_ctx, tq=512, kc=1024):
    B, W, S = qt.shape
    Lk = k_all.shape[1]
    hw = 2 * at_d
    n_heads = W // hw
    assert (Lk - n_ctx) % kc == 0 and n_ctx % V7X_LANES == 0
    chunks = tuple((i * kc, kc) for i in range((Lk - n_ctx) // kc)) + ((Lk - n_ctx, n_ctx),)
    qmap = lambda b, h, i: (b, h, i)
    omap = lambda b, h, i: (b, i, h)
    const = lambda b, h, i: (0, 0)
    return pl.pallas_call(
        functools.partial(_attn_kernel, lam_init=lam_init, at_d=at_d, chunks=chunks),
        grid=(B, n_heads, S // tq),
        in_specs=[pl.BlockSpec((1, hw, tq), qmap),
                  pl.BlockSpec((1, Lk, hw), lambda b, h, i: (b, 0, h)),
                  pl.BlockSpec((1, hw, Lk), lambda b, h, i: (b, h, 0)),
                  pl.BlockSpec((1, at_d), const), pl.BlockSpec((1, at_d), const),
                  pl.BlockSpec((1, at_d), const), pl.BlockSpec((1, at_d), const),
                  pl.BlockSpec((1, hw), const)],
        out_specs=pl.BlockSpec((1, tq, hw), omap),
        out_shape=jax.ShapeDtypeStruct((B, S, W), BF16),
        compiler_params=_cparams(3, VMEM_LIMIT),
        name="attn",
    )(qt, k_all, vt_all, lq1, lk1, lq2, lk2, subln_g)


def _dwconv_kernel(z_ref, w_ref, b_ref, o_ref):
    z = z_ref[0].astype(F32)
    L = z.shape[0]
    row = lax.broadcasted_iota(jnp.int32, z.shape, 0)
    zm = jnp.where(row == 0, 0.0, pltpu.roll(z, 1, 0))
    zp = jnp.where(row == L - 1, 0.0, pltpu.roll(z, L - 1, 0))
    o_ref[0] = (zm * w_ref[0:1, :] + z * w_ref[1:2, :] + zp * w_ref[2:3, :] + b_ref[...]).astype(o_ref.dtype)


def _dwconv(z, w, b, tc=512):
    B, L, C = z.shape
    return pl.pallas_call(
        _dwconv_kernel,
        grid=(B, C // tc),
        in_specs=[pl.BlockSpec((1, L, tc), lambda b, j: (b, 0, j)),
                  pl.BlockSpec((CONV_W, tc), lambda b, j: (0, j)),
                  pl.BlockSpec((1, tc), lambda b, j: (0, j))],
        out_specs=pl.BlockSpec((1, L, tc), lambda b, j: (b, 0, j)),
        out_shape=jax.ShapeDtypeStruct((B, L, C), BF16),
        compiler_params=_cparams(2, VMEM_LIMIT),
        name="dwconv",
    )(z, w, b)


def _filt_kernel(z_ref, w1_ref, b1_ref, w2_ref, b2_ref, w3_ref, b3_ref, fr_ref, wo_ref, dec_ref,
                 f_ref, b_ref, *, width, order):
    i = pl.program_id(0)
    hdot = lambda a, b: jnp.dot(a, b, preferred_element_type=F32, precision=HIGHEST)
    hdn = jnp.sin(fr_ref[0:1, :] * (hdot(z_ref[...], w1_ref[...]) + b1_ref[...]))
    hdn = jnp.sin(fr_ref[1:2, :] * (hdot(hdn, w2_ref[...]) + b2_ref[...]))
    hdn = jnp.sin(fr_ref[2:3, :] * (hdot(hdn, w3_ref[...]) + b3_ref[...]))
    h = hdot(hdn, wo_ref[...])
    dec = dec_ref[...]
    tl = dec.shape[0]
    row = lax.broadcasted_iota(jnp.int32, (tl, width), 0) + i * tl
    for o in range(order):
        f_ref[:, o * width:(o + 1) * width] = h[:, (2 * o) * width:(2 * o + 1) * width] * dec
        bwd = h[:, (2 * o + 1) * width:(2 * o + 2) * width] * dec
        b_ref[:, o * width:(o + 1) * width] = jnp.where(row == 0, 0.0, bwd)


def _filt(zemb, w1, b1, w2, b2, w3, b3, freq, wout, decay, width, order, tl=512):
    L, E = zemb.shape
    F = w2.shape[0]
    const = lambda i: (0, 0)
    rowm = lambda i: (i, 0)
    return pl.pallas_call(
        functools.partial(_filt_kernel, width=width, order=order),
        grid=(L // tl,),
        in_specs=[pl.BlockSpec((tl, E), rowm),
                  pl.BlockSpec((E, F), const), pl.BlockSpec((1, F), const),
                  pl.BlockSpec((F, F), const), pl.BlockSpec((1, F), const),
                  pl.BlockSpec((F, F), const), pl.BlockSpec((1, F), const),
                  pl.BlockSpec((3, F), const),
                  pl.BlockSpec((F, order * 2 * width), const),
                  pl.BlockSpec((tl, width), rowm)],
        out_specs=[pl.BlockSpec((tl, order * width), rowm), pl.BlockSpec((tl, order * width), rowm)],
        out_shape=[jax.ShapeDtypeStruct((L, order * width), F32), jax.ShapeDtypeStruct((L, order * width), F32)],
        compiler_params=_cparams(1, VMEM_LIMIT),
        name="filt",
    )(zemb, w1, b1, w2, b2, w3, b3, freq, wout, decay)


FFT_NB = 64
FFT_GROUP = 8


def _ld(ref, start, size, stride=None):
    idx = pl.ds(start, size) if stride is None else pl.ds(start, size, stride=stride)
    return jnp.concatenate([ref[t, idx, :] for t in range(ref.shape[0])], axis=1)


def _st(ref, start, size, val, stride=None):
    idx = pl.ds(start, size) if stride is None else pl.ds(start, size, stride=stride)
    for t in range(ref.shape[0]):
        ref[t, idx, :] = val[:, t * V7X_LANES:(t + 1) * V7X_LANES]


def _fft_stage_a(src_ref, s_ref, fa, *, a_len, kp):
    def body(g, carry):
        for j in range(FFT_GROUP):
            r = g * FFT_GROUP + j
            ur = _ld(src_ref, r, a_len, FFT_NB).astype(BF16)
            _st(s_ref, pl.multiple_of(r * (2 * kp), V7X_SUBLANES), 2 * kp, _dot(fa, ur))
        return carry
    lax.fori_loop(0, FFT_NB // FFT_GROUP, body, 0)


def _fft_stage_c(s_ref, gc_ref, k1, kp):
    sr = _ld(s_ref, k1, FFT_NB, 2 * kp)
    si = _ld(s_ref, kp + k1, FFT_NB, 2 * kp)
    x = _dot(gc_ref[k1], jnp.concatenate([sr, si], axis=0).astype(BF16))
    return x[:FFT_NB], x[FFT_NB:]


def _fspec_kernel(f_ref, b_ref, fa_ref, gc_ref, kr_ref, ki_ref, uf_ref, s_ref, *, a_len, kp):
    fa = fa_ref[...]

    def transform(src_ref, emit):
        _st(uf_ref, 0, uf_ref.shape[1], src_ref[...])
        _fft_stage_a(uf_ref, s_ref, fa, a_len=a_len, kp=kp)

        def body(g, carry):
            for j in range(FFT_GROUP):
                k1 = g * FFT_GROUP + j
                xr, xi = _fft_stage_c(s_ref, gc_ref, k1, kp)
                emit(k1, xr, xi)
            return carry
        lax.fori_loop(0, kp // FFT_GROUP, body, 0)

    def emit_fwd(k1, xr, xi):
        kr_ref[k1] = xr
        ki_ref[k1] = xi

    def emit_bwd(k1, xr, xi):
        kr_ref[k1] += xr
        ki_ref[k1] -= xi

    transform(f_ref, emit_fwd)
    transform(b_ref, emit_bwd)


def _fspec(f_taps, b_taps, fa, gc, a_len, kp, tc=256):
    L, W = f_taps.shape
    const2 = lambda j: (0, 0)
    const3 = lambda j: (0, 0, 0)
    col = lambda j: (0, j)
    return pl.pallas_call(
        functools.partial(_fspec_kernel, a_len=a_len, kp=kp),
        grid=(W // tc,),
        in_specs=[pl.BlockSpec((L, tc), col), pl.BlockSpec((L, tc), col),
                  pl.BlockSpec(fa.shape, const2), pl.BlockSpec(gc.shape, const3)],
        out_specs=[pl.BlockSpec((kp, FFT_NB, tc), lambda j: (0, 0, j)),
                   pl.BlockSpec((kp, FFT_NB, tc), lambda j: (0, 0, j))],
        out_shape=[jax.ShapeDtypeStruct((kp, FFT_NB, W), F32), jax.ShapeDtypeStruct((kp, FFT_NB, W), F32)],
        scratch_shapes=[pltpu.VMEM((tc // V7X_LANES, L, V7X_LANES), F32),
                        pltpu.VMEM((tc // V7X_LANES, FFT_NB * 2 * kp, V7X_LANES), F32)],
        compiler_params=_cparams(1, VMEM_LIMIT),
        name="fspec",
    )(f_taps, b_taps, fa, gc)


def _lconv_kernel(u_ref, gate_ref, fa_ref, fai_ref, gc_ref, gci_ref, kr_ref, ki_ref, bias_ref, o_ref,
                  uf_ref, s_ref, *, a_len, kp):
    _st(uf_ref, 0, uf_ref.shape[1], u_ref[0].astype(F32))
    _fft_stage_a(uf_ref, s_ref, fa_ref[...], a_len=a_len, kp=kp)

    def spectrum_product(g, carry):
        k1s = [g * FFT_GROUP + j for j in range(FFT_GROUP)]
        xs = [_fft_stage_c(s_ref, gc_ref, k1, kp) for k1 in k1s]
        outs = []
        for k1, (xr, xi) in zip(k1s, xs):
            kr, ki = kr_ref[k1], ki_ref[k1]
            y = jnp.concatenate([xr * kr - xi * ki, xr * ki + xi * kr], axis=0).astype(BF16)
            outs.append(_dot(gci_ref[k1], y))
        for k1, bc in zip(k1s, outs):
            _st(s_ref, k1, FFT_NB, bc[:FFT_NB], 2 * kp)
            _st(s_ref, kp + k1, FFT_NB, bc[FFT_NB:], 2 * kp)
        return carry
    lax.fori_loop(0, kp // FFT_GROUP, spectrum_product, 0)

    fai = fai_ref[...]

    def inverse_a(g, carry):
        offs = [pl.multiple_of((g * FFT_GROUP + j) * (2 * kp), V7X_SUBLANES) for j in range(FFT_GROUP)]
        slabs = [_ld(s_ref, off, 2 * kp).astype(BF16) for off in offs]
        for off, slab in zip(offs, slabs):
            _st(s_ref, off, a_len, _dot(fai, slab))
        return carry
    lax.fori_loop(0, FFT_NB // FFT_GROUP, inverse_a, 0)

    bias = bias_ref[...]

    def epilogue(g, carry):
        for j in range(FFT_GROUP):
            a = g * FFT_GROUP + j
            r0 = pl.multiple_of(a * FFT_NB, FFT_NB)
            y = _ld(s_ref, a, FFT_NB, 2 * kp)
            uf = _ld(uf_ref, r0, FFT_NB)
            gate = gate_ref[0, pl.ds(r0, FFT_NB), :].astype(F32)
            o_ref[0, pl.ds(r0, FFT_NB), :] = (gate * (y + uf * bias)).astype(o_ref.dtype)
        return carry
    lax.fori_loop(0, a_len // FFT_GROUP, epilogue, 0)


def _lconv(u_arr, u_col, gate_arr, gate_col, fa, fai, gc, gci, kr, ki, bias, order_idx, width, a_len, kp, tc=256):
    B, L, _ = u_arr.shape
    nh = width // tc
    const2 = lambda h, b: (0, 0)
    const3 = lambda h, b: (0, 0, 0)
    spec = lambda h, b: (0, 0, order_idx * nh + h)
    return pl.pallas_call(
        functools.partial(_lconv_kernel, a_len=a_len, kp=kp),
        grid=(nh, B),
        in_specs=[pl.BlockSpec((1, L, tc), lambda h, b: (b, 0, u_col * nh + h)),
                  pl.BlockSpec((1, L, tc), lambda h, b: (b, 0, gate_col * nh + h)),
                  pl.BlockSpec(fa.shape, const2), pl.BlockSpec(fai.shape, const2),
                  pl.BlockSpec(gc.shape, const3, pipeline_mode=pl.Buffered(1)),
                  pl.BlockSpec(gci.shape, const3, pipeline_mode=pl.Buffered(1)),
                  pl.BlockSpec((kp, FFT_NB, tc), spec, pipeline_mode=pl.Buffered(1)),
                  pl.BlockSpec((kp, FFT_NB, tc), spec, pipeline_mode=pl.Buffered(1)),
                  pl.BlockSpec((1, tc), lambda h, b: (0, h))],
        out_specs=pl.BlockSpec((1, L, tc), lambda h, b: (b, 0, h)),
        out_shape=jax.ShapeDtypeStruct((B, L, width), BF16),
        scratch_shapes=[pltpu.VMEM((tc // V7X_LANES, L, V7X_LANES), F32),
                        pltpu.VMEM((tc // V7X_LANES, FFT_NB * 2 * kp, V7X_LANES), F32)],
        compiler_params=_cparams(2, VMEM_LIMIT),
        name="lconv%d" % order_idx,
    )(u_arr, gate_arr, fa, fai, gc, gci, kr, ki, bias)


def _merge_kernel(x_ref, yhy_ref, oat_ref, g_ref, g1_ref, lng_ref, lnb_ref, l1g_ref, l1b_ref,
                  whyo_ref, wato_ref, wout_ref, o_ref, *, alpha):
    d = whyo_ref.shape[1]
    xln = _layer_norm(x_ref[0], lng_ref[...], lnb_ref[...])
    g = g_ref[0].astype(F32)
    m = (_sigmoid(g[:, :d]) * _dot(yhy_ref[0], whyo_ref[...])
         + _sigmoid(g[:, d:]) * _dot(oat_ref[0], wato_ref[...]))
    y = _dot(m.astype(BF16), wout_ref[...])
    o_ref[0] = _layer_norm(alpha * xln + g1_ref[0] * y, l1g_ref[...], l1b_ref[...])


def _merge(x, yhy, oat, g, g1, lng, lnb, l1g, l1b, whyo, wato, wout, alpha, tm=512):
    B, S, D = x.shape
    row = lambda b, i: (b, i, 0)
    mod = lambda b, i: (b, 0, 0)
    const = lambda b, i: (0, 0)
    vec = pl.BlockSpec((1, D), const)
    return pl.pallas_call(
        functools.partial(_merge_kernel, alpha=alpha),
        grid=(B, S // tm),
        in_specs=[pl.BlockSpec((1, tm, D), row),
                  pl.BlockSpec((1, tm, yhy.shape[2]), row), pl.BlockSpec((1, tm, oat.shape[2]), row),
                  pl.BlockSpec((1, tm, g.shape[2]), row),
                  pl.BlockSpec((1, 1, D), mod), vec, vec, vec, vec,
                  pl.BlockSpec(whyo.shape, const), pl.BlockSpec(wato.shape, const), pl.BlockSpec(wout.shape, const)],
        out_specs=pl.BlockSpec((1, tm, D), row),
        out_shape=jax.ShapeDtypeStruct((B, S, D), F32),
        compiler_params=_cparams(2, VMEM_LIMIT),
        name="merge",
    )(x, yhy, oat, g, g1, lng, lnb, l1g, l1b, whyo, wato, wout)


def _ffn_kernel(x_ref, xp_ref, xn_ref, sh_ref, sc_ref, g2_ref, wup_ref, cw_ref, cb_ref, wdn_ref,
                l2g_ref, l2b_ref, o_ref, *, alpha, d_ff, tc, halo):
    i = pl.program_id(1)
    n_i = pl.num_programs(1)
    x = x_ref[0]
    tm = x.shape[0]
    sc = 1.0 + sc_ref[0]
    sh = sh_ref[0]
    hp = jnp.where(i > 0, xp_ref[0] * sc + sh, 0.0)
    hn = jnp.where(i < n_i - 1, xn_ref[0] * sc + sh, 0.0)
    h = jnp.concatenate([hp, x * sc + sh, hn], axis=0).astype(BF16)
    rows = tm + 2 * halo

    def conv(u, lo, sz):
        w = cw_ref[:, lo:lo + sz]
        um = pltpu.roll(u, 1, 0)[halo:halo + tm]
        up = pltpu.roll(u, rows - 1, 0)[halo:halo + tm]
        return um * w[0:1] + u[halo:halo + tm] * w[1:2] + up * w[2:3] + cb_ref[:, lo:lo + sz]

    def up(lo, sz):
        return _dot(h, wup_ref[:, lo:lo + sz]), _dot(h, wup_ref[:, d_ff + lo:d_ff + lo + sz])

    chunks = [(lo, min(tc, d_ff - lo)) for lo in range(0, d_ff, tc)]
    acc = jnp.zeros((tm, o_ref.shape[2]), F32)
    cur = up(*chunks[0])
    for c, (lo, sz) in enumerate(chunks):
        nxt = up(*chunks[c + 1]) if c + 1 < len(chunks) else None
        a = conv(cur[0], lo, sz)
        g = conv(cur[1], d_ff + lo, sz)
        act = (g * _sigmoid(g) * a).astype(BF16)
        acc = acc + _dot(act, wdn_ref[lo:lo + sz, :])
        cur = nxt
    o_ref[0] = _layer_norm(alpha * x + g2_ref[0] * acc, l2g_ref[...], l2b_ref[...])


def _ffn(x1, sh, sc, g2, wup, cw, cb, wdn, l2g, l2b, alpha, tm=512, tc=512):
    B, S, D = x1.shape
    d_ff = wdn.shape[0]
    halo = V7X_SUBLANES
    nb = tm // halo
    row = lambda b, i: (b, i, 0)
    prev = lambda b, i: (b, jnp.maximum(i * nb - 1, 0), 0)
    nxt = lambda b, i: (b, jnp.minimum((i + 1) * nb, S // halo - 1), 0)
    mod = lambda b, i: (b, 0, 0)
    const = lambda b, i: (0, 0)
    vec = pl.BlockSpec((1, D), const)
    return pl.pallas_call(
        functools.partial(_ffn_kernel, alpha=alpha, d_ff=d_ff, tc=tc, halo=halo),
        grid=(B, S // tm),
        in_specs=[pl.BlockSpec((1, tm, D), row), pl.BlockSpec((1, halo, D), prev), pl.BlockSpec((1, halo, D), nxt),
                  pl.BlockSpec((1, 1, D), mod), pl.BlockSpec((1, 1, D), mod), pl.BlockSpec((1, 1, D), mod),
                  pl.BlockSpec(wup.shape, const, pipeline_mode=pl.Buffered(1)),
                  pl.BlockSpec(cw.shape, const), pl.BlockSpec(cb.shape, const),
                  pl.BlockSpec(wdn.shape, const, pipeline_mode=pl.Buffered(1)),
                  vec, vec],
        out_specs=pl.BlockSpec((1, tm, D), row),
        out_shape=jax.ShapeDtypeStruct((B, S, D), F32),
        compiler_params=_cparams(2, VMEM_LIMIT),
        name="ffn",
    )(x1, x1, x1, sh, sc, g2, wup, cw, cb, wdn, l2g, l2b)


def _rope_tables(S, at_d):
    rope_axis = at_d // 2
    half = rope_axis // 2
    pos = np.arange(S)
    rowp = (pos // GRID_W).astype(np.float64)
    colp = (pos % GRID_W).astype(np.float64)
    inv = ROPE_BASE ** (-np.arange(0, rope_axis, 2, dtype=np.float64) / rope_axis)
    lane = np.arange(V7X_LANES)
    d = lane % at_d
    e = d % rope_axis
    f = e % half
    ang = np.where((d < rope_axis)[None, :], rowp[:, None], colp[:, None]) * inv[f][None, :]
    first = (e < half)[None, :]
    sin = np.sin(ang)
    as_f32 = lambda m: jnp.asarray(m, dtype=F32)
    return as_f32(np.cos(ang)), as_f32(np.where(first, -sin, 0.0)), as_f32(np.where(first, 0.0, sin))


def _filter_tables(L, emb, width):
    bands = (emb - 1) // 2
    t = np.linspace(0.0, 1.0, L)[:, None]
    w = (2.0 * np.pi / L) * np.arange(L, dtype=np.float64)[:, None]
    f = np.linspace(1e-4, bands - 1, bands)[None, :]
    z = np.concatenate([t, np.cos(f * w), -np.sin(f * w)], -1)
    min_decay = math.log(HY_DECAY_TARGET) / HY_SLOW_PCT
    max_decay = math.log(HY_DECAY_TARGET) / HY_FAST_PCT
    deltas = np.linspace(min_decay, max_decay, width)
    decay = np.exp(-t * np.abs(deltas)[None, :])
    epad = (-emb) % V7X_LANES
    z = np.pad(z, ((0, 0), (0, epad)))
    return jnp.asarray(z, dtype=F32), jnp.asarray(decay, dtype=F32)


def _fft_tables(L):
    a_len = L // FFT_NB
    assert a_len * FFT_NB == L and a_len % FFT_GROUP == 0
    n_fft, n1 = 2 * L, 2 * a_len
    kp = -(-(a_len + 1) // FFT_GROUP) * FFT_GROUP
    k1 = np.arange(kp)
    valid = (k1 <= a_len).astype(np.float64)
    ph = (np.outer(k1, np.arange(a_len)) % n1) * (2.0 * np.pi / n1)
    ca, sa = np.cos(ph) * valid[:, None], np.sin(ph) * valid[:, None]
    fa = np.concatenate([ca, -sa], axis=0)
    w = np.where((k1 == 0) | (k1 == a_len), 1.0, 2.0) / n_fft
    fai = np.concatenate([(ca * w[:, None]).T, (-sa * w[:, None]).T], axis=1)
    r = np.arange(FFT_NB)
    kk = k1[:, None, None] + n1 * r[None, :, None]
    th = ((kk * r[None, None, :]) % n_fft) * (2.0 * np.pi / n_fft)
    gr, gi = np.cos(th) * valid[:, None, None], -np.sin(th) * valid[:, None, None]
    gc = np.concatenate([np.concatenate([gr, -gi], 2), np.concatenate([gi, gr], 2)], 1)
    grt, git = gr.transpose(0, 2, 1), gi.transpose(0, 2, 1)
    gci = np.concatenate([np.concatenate([grt, git], 2), np.concatenate([-git, grt], 2)], 1)
    as_bf16 = lambda m: jnp.asarray(m, dtype=F32).astype(BF16)
    return a_len, kp, as_bf16(fa), as_bf16(fai), as_bf16(gc), as_bf16(gci)


def kernel(x, c, ctx, c_ctx, ln_in_g, ln_in_b, w_ada, b_ada, w_in, hy_conv_w, hy_conv_b, hy_f_w1, hy_f_b1, hy_f_w2, hy_f_b2, hy_f_w3, hy_f_b3, hy_f_freq, hy_f_wout, hy_bias, lam_q1, lam_k1, lam_q2, lam_k2, at_subln_g, w_hy_o, w_at_o, w_out, ln1_g, ln1_b, ffn_w_up, ffn_conv_w, ffn_conv_b, ffn_w_down, ln2_g, ln2_b):
    B, S, D = x.shape
    depth = w_ada.shape[0]
    assert depth == 1, "single-layer configuration only"
    l = 0
    order, width = hy_bias.shape[1], hy_bias.shape[2]
    at_d = lam_q1.shape[1]
    at_w = w_at_o.shape[1]
    hy_cols = (order + 1) * width
    emb = hy_f_w1.shape[1]
    alpha = (2.0 * depth) ** 0.25
    lam_init = 0.8 - 0.6 * math.exp(-0.3 * l)
    assert 2 * at_d == V7X_LANES and order == 2

    row2 = lambda a: a.reshape(1, -1)

    pad = (-(B + 1)) % V7X_SUBLANES
    cc = jnp.concatenate([c, c_ctx[None, :], jnp.zeros((pad, D), F32)], 0)
    mod = _ada(cc, w_ada[l], row2(b_ada[l]))
    sh1, sc1, g1, sh2, sc2, g2 = [mod[:B, i * D:(i + 1) * D].reshape(B, 1, D) for i in range(N_ADA)]
    sh1c, sc1c = mod[B:B + 1, 0:D], mod[B:B + 1, D:2 * D]

    lng, lnb = row2(ln_in_g), row2(ln_in_b)
    w_in_b = w_in[l].astype(BF16)
    cos, sin_a, sin_b = _rope_tables(S, at_d)
    z_hy, qt, k_l, vt_l, gates = _inproj(x, sh1, sc1, lng, lnb, w_in_b, cos, sin_a, sin_b, hy_cols, at_w, at_d,
                                         ctx.shape[1])
    k_all, vt_all = _ctx_kv(ctx, sh1c, sc1c, lng, lnb, w_in_b[:, hy_cols + at_w:hy_cols + 3 * at_w],
                            k_l, vt_l, at_w)
    o_at = _attn(qt, k_all, vt_all, row2(lam_q1[l]), row2(lam_k1[l]), row2(lam_q2[l]), row2(lam_k2[l]),
                 row2(at_subln_g[l]), lam_init, at_d, ctx.shape[1])

    u3 = _dwconv(z_hy, hy_conv_w[l], row2(hy_conv_b[l]))
    zemb, decay = _filter_tables(S, emb, width)
    w1p = jnp.pad(hy_f_w1[l], ((0, zemb.shape[1] - emb), (0, 0)))
    f_taps, b_taps = _filt(zemb, w1p, row2(hy_f_b1[l]), hy_f_w2[l], row2(hy_f_b2[l]), hy_f_w3[l],
                           row2(hy_f_b3[l]), hy_f_freq[l], hy_f_wout[l], decay, width, order)
    a_len, kp, fa, fai, gc, gci = _fft_tables(S)
    kr, ki = _fspec(f_taps, b_taps, fa, gc, a_len, kp)
    zz = _lconv(u3, 0, u3, 1, fa, fai, gc, gci, kr, ki, row2(hy_bias[l, 0]), 0, width, a_len, kp)
    y_hy = _lconv(zz, 0, u3, 2, fa, fai, gc, gci, kr, ki, row2(hy_bias[l, 1]), 1, width, a_len, kp)

    x1 = _merge(x, y_hy, o_at, gates, g1, lng, lnb, row2(ln1_g[l]), row2(ln1_b[l]),
                w_hy_o[l].astype(BF16), w_at_o[l].astype(BF16), w_out[l].astype(BF16), alpha)
    return _ffn(x1, sh2, sc2, g2, ffn_w_up[l].astype(BF16), ffn_conv_w[l], row2(ffn_conv_b[l]),
                ffn_w_down[l].astype(BF16), row2(ln2_g[l]), row2(ln2_b[l]), alpha)
```

```python
import functools
import math

import numpy as np
import jax
import jax.numpy as jnp
from jax import lax
from jax.experimental import pallas as pl
from jax.experimental.pallas import tpu as pltpu

F32 = jnp.float32
BF16 = jnp.bfloat16

LN_EPS = 1e-5
RMS_EPS = 1e-5
GRID_W = 64
ROPE_BASE = 10000.0
N_ADA = 6
HY_DECAY_TARGET = 1e-2
HY_FAST_PCT = 0.3
HY_SLOW_PCT = 1.5

V7X_LANES = 128
V7X_SUBLANES = 8
V7X_VMEM_BYTES = 64 * 1024 * 1024
VMEM_LIMIT = 56 * 1024 * 1024

HIGHEST = lax.Precision.HIGHEST


def _cparams(n_axes, vmem=None):
    return pltpu.CompilerParams(dimension_semantics=("arbitrary",) * n_axes, vmem_limit_bytes=vmem)


def _layer_norm(x, g, b):
    mu = jnp.mean(x, axis=-1, keepdims=True)
    xc = x - mu
    var = jnp.mean(xc * xc, axis=-1, keepdims=True)
    return xc * lax.rsqrt(var + LN_EPS) * g + b


def _sigmoid(x):
    return 1.0 / (1.0 + jnp.exp(-x))


def _dot(a, b):
    return jnp.dot(a, b, preferred_element_type=F32)


def _ada_kernel(c_ref, w_ref, b_ref, o_ref):
    c = c_ref[...]
    s = c * _sigmoid(c)
    o_ref[...] = jnp.dot(s, w_ref[...], preferred_element_type=F32, precision=HIGHEST) + b_ref[...]


def _ada(cc, w, b, tn=1536):
    rows, d = cc.shape
    n = w.shape[1]
    return pl.pallas_call(
        _ada_kernel,
        grid=(n // tn,),
        in_specs=[pl.BlockSpec((rows, d), lambda j: (0, 0)),
                  pl.BlockSpec((d, tn), lambda j: (0, j)),
                  pl.BlockSpec((1, tn), lambda j: (0, j))],
        out_specs=pl.BlockSpec((rows, tn), lambda j: (0, j)),
        out_shape=jax.ShapeDtypeStruct((rows, n), F32),
        compiler_params=_cparams(1, VMEM_LIMIT),
        name="ada",
    )(cc, w, b)


def _rope(x, cos, sin_a, sin_b, n_heads):
    outs = []
    for h in range(n_heads):
        xh = x[:, h * V7X_LANES:(h + 1) * V7X_LANES]
        up = pltpu.roll(xh, V7X_LANES - 16, 1)
        dn = pltpu.roll(xh, 16, 1)
        outs.append(xh * cos + up * sin_a + dn * sin_b)
    return jnp.concatenate(outs, axis=1)


def _inproj_kernel(x_ref, xp_ref, xn_ref, sh_ref, sc_ref, lng_ref, lnb_ref, w_ref, cw_ref, cb_ref,
                   cos_ref, sa_ref, sb_ref, hy_ref, q_ref, k_ref, v_ref, g_ref, *, hy_cols, at_w, q_scale, halo, tc):
    i = pl.program_id(0)
    n_i = pl.num_programs(0)
    lng, lnb, sc, sh = lng_ref[...], lnb_ref[...], 1.0 + sc_ref[0], sh_ref[0]
    mod = lambda xr: _layer_norm(xr, lng, lnb) * sc + sh
    h = mod(x_ref[0]).astype(BF16)
    tm = h.shape[0]
    hp = jnp.where(i > 0, mod(xp_ref[0]), 0.0).astype(BF16)
    hn = jnp.where(i < n_i - 1, mod(xn_ref[0]), 0.0).astype(BF16)
    h_cat = jnp.concatenate([hp, h, hn], axis=0)
    rows = tm + 2 * halo
    for lo in range(0, hy_cols, tc):
        z = _dot(h_cat, w_ref[:, lo:lo + tc])
        w = cw_ref[:, lo:lo + tc]
        zm = pltpu.roll(z, 1, 0)[halo:halo + tm]
        zp = pltpu.roll(z, rows - 1, 0)[halo:halo + tm]
        u = zm * w[0:1] + z[halo:halo + tm] * w[1:2] + zp * w[2:3] + cb_ref[:, lo:lo + tc]
        hy_ref[0, :, lo:lo + tc] = u.astype(hy_ref.dtype)
    n_heads = at_w // V7X_LANES
    o1 = hy_cols
    o2, o3, o4 = o1 + at_w, o1 + 2 * at_w, o1 + 3 * at_w
    cos, sa, sb = cos_ref[...], sa_ref[...], sb_ref[...]
    q = _rope(_dot(h, w_ref[:, o1:o2]), cos, sa, sb, n_heads)
    q_ref[0] = (q * q_scale).T.astype(q_ref.dtype)
    k = _rope(_dot(h, w_ref[:, o2:o3]), cos, sa, sb, n_heads)
    k_ref[0] = k.astype(k_ref.dtype)
    v_ref[0] = _dot(h, w_ref[:, o3:o4]).T.astype(v_ref.dtype)
    g_ref[0] = _dot(h, w_ref[:, o4:]).astype(g_ref.dtype)


def _inproj(x, sh, sc, lng, lnb, w, cw, cb, cos, sa, sb, hy_cols, at_w, at_d, tm=512):
    B, S, D = x.shape
    ncols = w.shape[1]
    g_cols = ncols - hy_cols - 3 * at_w
    halo = 2 * V7X_SUBLANES
    nb = tm // halo
    kern = functools.partial(_inproj_kernel, hy_cols=hy_cols, at_w=at_w, q_scale=at_d ** -0.5 * math.log2(math.e),
                             halo=halo, tc=512)
    row = lambda i, b: (b, i, 0)
    prev = lambda i, b: (b, jnp.maximum(i * nb - 1, 0), 0)
    nxt = lambda i, b: (b, jnp.minimum((i + 1) * nb, S // halo - 1), 0)
    rowt = lambda i, b: (b, 0, i)
    mod = lambda i, b: (b, 0, 0)
    const = lambda i, b: (0, 0)
    tab = lambda i, b: (i, 0)
    return pl.pallas_call(
        kern,
        grid=(S // tm, B),
        in_specs=[pl.BlockSpec((1, tm, D), row), pl.BlockSpec((1, halo, D), prev), pl.BlockSpec((1, halo, D), nxt),
                  pl.BlockSpec((1, 1, D), mod), pl.BlockSpec((1, 1, D), mod),
                  pl.BlockSpec((1, D), const), pl.BlockSpec((1, D), const),
                  pl.BlockSpec((D, ncols), const, pipeline_mode=pl.Buffered(1)),
                  pl.BlockSpec(cw.shape, const), pl.BlockSpec(cb.shape, const),
                  pl.BlockSpec((tm, V7X_LANES), tab), pl.BlockSpec((tm, V7X_LANES), tab),
                  pl.BlockSpec((tm, V7X_LANES), tab)],
        out_specs=[pl.BlockSpec((1, tm, hy_cols), row), pl.BlockSpec((1, at_w, tm), rowt),
                   pl.BlockSpec((1, tm, at_w), row), pl.BlockSpec((1, at_w, tm), rowt),
                   pl.BlockSpec((1, tm, g_cols), row)],
        out_shape=[jax.ShapeDtypeStruct((B, S, hy_cols), BF16), jax.ShapeDtypeStruct((B, at_w, S), BF16),
                   jax.ShapeDtypeStruct((B, S, at_w), BF16), jax.ShapeDtypeStruct((B, at_w, S), BF16),
                   jax.ShapeDtypeStruct((B, S, g_cols), BF16)],
        compiler_params=_cparams(2, VMEM_LIMIT),
        name="inproj",
    )(x, x, x, sh, sc, lng, lnb, w, cw, cb, cos, sa, sb)


def _ctx_kv_kernel(x_ref, sh_ref, sc_ref, lng_ref, lnb_ref, w_ref, k_ref, v_ref, *, at_w):
    xn = _layer_norm(x_ref[0], lng_ref[...], lnb_ref[...])
    h = (xn * (1.0 + sc_ref[...]) + sh_ref[...]).astype(BF16)
    k_ref[0] = _dot(h, w_ref[:, 0:at_w]).astype(k_ref.dtype)
    v_ref[0] = _dot(h, w_ref[:, at_w:]).T.astype(v_ref.dtype)


def _ctx_kv(ctx, sh, sc, lng, lnb, w_kv, at_w):
    B, C, D = ctx.shape
    const = lambda b: (0, 0)
    row = lambda b: (b, 0, 0)
    return pl.pallas_call(
        functools.partial(_ctx_kv_kernel, at_w=at_w),
        grid=(B,),
        in_specs=[pl.BlockSpec((1, C, D), row),
                  pl.BlockSpec((1, D), const), pl.BlockSpec((1, D), const),
                  pl.BlockSpec((1, D), const), pl.BlockSpec((1, D), const),
                  pl.BlockSpec((D, 2 * at_w), const)],
        out_specs=[pl.BlockSpec((1, C, at_w), row), pl.BlockSpec((1, at_w, C), row)],
        out_shape=[jax.ShapeDtypeStruct((B, C, at_w), BF16), jax.ShapeDtypeStruct((B, at_w, C), BF16)],
        compiler_params=_cparams(1),
        name="ctx_kv",
    )(ctx, sh, sc, lng, lnb, w_kv)


ATTN_AHEAD = 2


def _attn_kernel(qt_ref, k_ref, vt_ref, kc_ref, vtc_ref, lq1_ref, lk1_ref, lq2_ref, lk2_ref, g_ref, o_ref, *,
                 lam_init, at_d, kc):
    qt = qt_ref[0]
    hw, tq = qt.shape
    lam = (jnp.exp(jnp.sum(lq1_ref[...] * lk1_ref[...], axis=-1, keepdims=True))
           - jnp.exp(jnp.sum(lq2_ref[...] * lk2_ref[...], axis=-1, keepdims=True)) + lam_init)
    row = lax.broadcasted_iota(jnp.int32, qt.shape, 0)
    zero = jnp.zeros_like(qt)
    qts = (jnp.where(row < at_d, qt, zero), jnp.where(row >= at_d, qt, zero))
    state = [(jnp.full((1, tq), -1e30, F32), jnp.zeros((1, tq), F32), jnp.zeros((hw, tq), F32))
             for _ in range(2)]
    chunks = [(k_ref, vt_ref, off, kc) for off in range(0, k_ref.shape[1], kc)]
    chunks.append((kc_ref, vtc_ref, 0, kc_ref.shape[1]))
    units = [chunk + (comp,) for chunk in chunks for comp in range(2)]

    def scores(unit):
        kref, _, off, size, comp = unit
        return _dot(kref[0, off:off + size, :], qts[comp])

    pending = [scores(u) for u in units[:ATTN_AHEAD]]
    for i, (_, vref, off, size, comp) in enumerate(units):
        if i + ATTN_AHEAD < len(units):
            pending.append(scores(units[i + ATTN_AHEAD]))
        st = pending.pop(0)
        m, l, acc = state[comp]
        m_new = jnp.maximum(m, jnp.max(st, axis=0, keepdims=True))
        alpha = jnp.exp2(m - m_new)
        p = jnp.exp2(st - m_new)
        l = alpha * l + jnp.sum(p, axis=0, keepdims=True)
        acc = alpha * acc + _dot(vref[0, :, off:off + size], p.astype(BF16))
        state[comp] = (m_new, l, acc)
    (_, l1, a1), (_, l2, a2) = state
    ot = a1 * (1.0 / l1) - a2 * (lam / l2)
    ot = ot * lax.rsqrt(jnp.mean(ot * ot, axis=0, keepdims=True) + RMS_EPS)
    o_ref[0] = (ot.T * g_ref[...] * (1.0 - lam_init)).astype(o_ref.dtype)


def _attn(qt, k_lat, vt_lat, k_ctx, vt_ctx, lq1, lk1, lq2, lk2, subln_g, lam_init, at_d, tq=512, kc=1024):
    B, W, S = qt.shape
    n_ctx = k_ctx.shape[1]
    hw = 2 * at_d
    n_heads = W // hw
    assert S % kc == 0
    qmap = lambda b, h, i: (b, h, i)
    omap = lambda b, h, i: (b, i, h)
    kmap = lambda b, h, i: (b, 0, h)
    vmap = lambda b, h, i: (b, h, 0)
    const = lambda b, h, i: (0, 0)
    return pl.pallas_call(
        functools.partial(_attn_kernel, lam_init=lam_init, at_d=at_d, kc=kc),
        grid=(B, n_heads, S // tq),
        in_specs=[pl.BlockSpec((1, hw, tq), qmap),
                  pl.BlockSpec((1, S, hw), kmap), pl.BlockSpec((1, hw, S), vmap),
                  pl.BlockSpec((1, n_ctx, hw), kmap), pl.BlockSpec((1, hw, n_ctx), vmap),
                  pl.BlockSpec((1, at_d), const), pl.BlockSpec((1, at_d), const),
                  pl.BlockSpec((1, at_d), const), pl.BlockSpec((1, at_d), const),
                  pl.BlockSpec((1, hw), const)],
        out_specs=pl.BlockSpec((1, tq, hw), omap),
        out_shape=jax.ShapeDtypeStruct((B, S, W), BF16),
        compiler_params=_cparams(3, VMEM_LIMIT),
        name="attn",
    )(qt, k_lat, vt_lat, k_ctx, vt_ctx, lq1, lk1, lq2, lk2, subln_g)


def _filt_kernel(z_ref, w1_ref, b1_ref, w2_ref, b2_ref, w3_ref, b3_ref, fr_ref, wo_ref, dec_ref,
                 f_ref, b_ref, *, width, order):
    i = pl.program_id(0)
    hdot = lambda a, b: jnp.dot(a, b, preferred_element_type=F32, precision=HIGHEST)
    hdn = jnp.sin(fr_ref[0:1, :] * (hdot(z_ref[...], w1_ref[...]) + b1_ref[...]))
    hdn = jnp.sin(fr_ref[1:2, :] * (hdot(hdn, w2_ref[...]) + b2_ref[...]))
    hdn = jnp.sin(fr_ref[2:3, :] * (hdot(hdn, w3_ref[...]) + b3_ref[...]))
    h = hdot(hdn, wo_ref[...])
    dec = dec_ref[...]
    tl = dec.shape[0]
    row = lax.broadcasted_iota(jnp.int32, (tl, width), 0) + i * tl
    for o in range(order):
        f_ref[:, o * width:(o + 1) * width] = h[:, (2 * o) * width:(2 * o + 1) * width] * dec
        bwd = h[:, (2 * o + 1) * width:(2 * o + 2) * width] * dec
        b_ref[:, o * width:(o + 1) * width] = jnp.where(row == 0, 0.0, bwd)


def _filt(zemb, w1, b1, w2, b2, w3, b3, freq, wout, decay, width, order, tl=512):
    L, E = zemb.shape
    F = w2.shape[0]
    const = lambda i: (0, 0)
    rowm = lambda i: (i, 0)
    return pl.pallas_call(
        functools.partial(_filt_kernel, width=width, order=order),
        grid=(L // tl,),
        in_specs=[pl.BlockSpec((tl, E), rowm),
                  pl.BlockSpec((E, F), const), pl.BlockSpec((1, F), const),
                  pl.BlockSpec((F, F), const), pl.BlockSpec((1, F), const),
                  pl.BlockSpec((F, F), const), pl.BlockSpec((1, F), const),
                  pl.BlockSpec((3, F), const),
                  pl.BlockSpec((F, order * 2 * width), const),
                  pl.BlockSpec((tl, width), rowm)],
        out_specs=[pl.BlockSpec((tl, order * width), rowm), pl.BlockSpec((tl, order * width), rowm)],
        out_shape=[jax.ShapeDtypeStruct((L, order * width), F32), jax.ShapeDtypeStruct((L, order * width), F32)],
        compiler_params=_cparams(1, VMEM_LIMIT),
        name="filt",
    )(zemb, w1, b1, w2, b2, w3, b3, freq, wout, decay)


FFT_NB = 64
FFT_GROUP = 8


def _pitch(rows):
    p = -(-rows // V7X_SUBLANES)
    return (p + 1 - p % 2) * V7X_SUBLANES


def _ld(ref, start, size, stride=None):
    idx = pl.ds(start, size) if stride is None else pl.ds(start, size, stride=stride)
    return jnp.concatenate([ref[t, idx, :] for t in range(ref.shape[0])], axis=1)


def _st(ref, start, size, val, stride=None):
    idx = pl.ds(start, size) if stride is None else pl.ds(start, size, stride=stride)
    for t in range(ref.shape[0]):
        ref[t, idx, :] = val[:, t * V7X_LANES:(t + 1) * V7X_LANES]


def _stage_rows(src, uf_ref, a_len):
    up = _pitch(FFT_NB)

    def body(a, carry):
        _st(uf_ref, pl.multiple_of(a * up, V7X_SUBLANES), FFT_NB, src(pl.multiple_of(a * FFT_NB, FFT_NB)))
        return carry
    lax.fori_loop(0, a_len, body, 0)


def _fft_stage_a(uf_ref, s_ref, fa, *, a_len, kp):
    up, sp = _pitch(FFT_NB), _pitch(2 * kp)

    def body(g, carry):
        for j in range(FFT_GROUP):
            r = g * FFT_GROUP + j
            ur = _ld(uf_ref, r, a_len, up).astype(BF16)
            _st(s_ref, pl.multiple_of(r * sp, V7X_SUBLANES), 2 * kp, _dot(fa, ur))
        return carry
    lax.fori_loop(0, FFT_NB // FFT_GROUP, body, 0)


def _fft_stage_c(s_ref, gc_ref, k1, kp):
    sp = _pitch(2 * kp)
    sr = _ld(s_ref, k1, FFT_NB, sp)
    si = _ld(s_ref, kp + k1, FFT_NB, sp)
    x = _dot(gc_ref[k1], jnp.concatenate([sr, si], axis=0).astype(BF16))
    return x[:FFT_NB], x[FFT_NB:]


def _fspec_kernel(f_ref, b_ref, fa_ref, gc_ref, kr_ref, ki_ref, uf_ref, s_ref, *, a_len, kp):
    fa = fa_ref[...]

    def transform(src_ref, emit):
        _stage_rows(lambda r0: src_ref[pl.ds(r0, FFT_NB), :], uf_ref, a_len)
        _fft_stage_a(uf_ref, s_ref, fa, a_len=a_len, kp=kp)

        def body(g, carry):
            for j in range(FFT_GROUP):
                k1 = g * FFT_GROUP + j
                xr, xi = _fft_stage_c(s_ref, gc_ref, k1, kp)
                emit(k1, xr, xi)
            return carry
        lax.fori_loop(0, kp // FFT_GROUP, body, 0)

    def emit_fwd(k1, xr, xi):
        kr_ref[k1] = xr
        ki_ref[k1] = xi

    def emit_bwd(k1, xr, xi):
        kr_ref[k1] += xr
        ki_ref[k1] -= xi

    transform(f_ref, emit_fwd)
    transform(b_ref, emit_bwd)


def _fspec(f_taps, b_taps, fa, gc, a_len, kp, tc=256):
    L, W = f_taps.shape
    const2 = lambda j: (0, 0)
    const3 = lambda j: (0, 0, 0)
    col = lambda j: (0, j)
    return pl.pallas_call(
        functools.partial(_fspec_kernel, a_len=a_len, kp=kp),
        grid=(W // tc,),
        in_specs=[pl.BlockSpec((L, tc), col), pl.BlockSpec((L, tc), col),
                  pl.BlockSpec(fa.shape, const2), pl.BlockSpec(gc.shape, const3)],
        out_specs=[pl.BlockSpec((kp, FFT_NB, tc), lambda j: (0, 0, j)),
                   pl.BlockSpec((kp, FFT_NB, tc), lambda j: (0, 0, j))],
        out_shape=[jax.ShapeDtypeStruct((kp, FFT_NB, W), F32), jax.ShapeDtypeStruct((kp, FFT_NB, W), F32)],
        scratch_shapes=[pltpu.VMEM((tc // V7X_LANES, a_len * _pitch(FFT_NB), V7X_LANES), F32),
                        pltpu.VMEM((tc // V7X_LANES, FFT_NB * _pitch(2 * kp), V7X_LANES), F32)],
        compiler_params=_cparams(1, VMEM_LIMIT),
        name="fspec",
    )(f_taps, b_taps, fa, gc)


def _lconv_kernel(u_ref, gate_ref, fa_ref, fai_ref, gc_ref, gci_ref, kr_ref, ki_ref, bias_ref, o_ref,
                  uf_ref, s_ref, *, a_len, kp):
    up, sp = _pitch(FFT_NB), _pitch(2 * kp)
    _stage_rows(lambda r0: u_ref[0, pl.ds(r0, FFT_NB), :].astype(F32), uf_ref, a_len)
    _fft_stage_a(uf_ref, s_ref, fa_ref[...], a_len=a_len, kp=kp)

    def spectrum_product(g, carry):
        k1s = [g * FFT_GROUP + j for j in range(FFT_GROUP)]
        xs = [_fft_stage_c(s_ref, gc_ref, k1, kp) for k1 in k1s]
        outs = []
        for k1, (xr, xi) in zip(k1s, xs):
            kr, ki = kr_ref[k1], ki_ref[k1]
            y = jnp.concatenate([xr * kr - xi * ki, xr * ki + xi * kr], axis=0).astype(BF16)
            outs.append(_dot(gci_ref[k1], y))
        for k1, bc in zip(k1s, outs):
            _st(s_ref, k1, FFT_NB, bc[:FFT_NB], sp)
            _st(s_ref, kp + k1, FFT_NB, bc[FFT_NB:], sp)
        return carry
    lax.fori_loop(0, kp // FFT_GROUP, spectrum_product, 0)

    fai = fai_ref[...]

    def inverse_a(g, carry):
        offs = [pl.multiple_of((g * FFT_GROUP + j) * sp, V7X_SUBLANES) for j in range(FFT_GROUP)]
        slabs = [_ld(s_ref, off, 2 * kp).astype(BF16) for off in offs]
        for off, slab in zip(offs, slabs):
            _st(s_ref, off, a_len, _dot(fai, slab))
        return carry
    lax.fori_loop(0, FFT_NB // FFT_GROUP, inverse_a, 0)

    bias = bias_ref[...]

    def epilogue(g, carry):
        for j in range(FFT_GROUP):
            a = g * FFT_GROUP + j
            r0 = pl.multiple_of(a * FFT_NB, FFT_NB)
            y = _ld(s_ref, a, FFT_NB, sp)
            uf = _ld(uf_ref, pl.multiple_of(a * up, V7X_SUBLANES), FFT_NB)
            gate = gate_ref[0, pl.ds(r0, FFT_NB), :].astype(F32)
            o_ref[0, pl.ds(r0, FFT_NB), :] = (gate * (y + uf * bias)).astype(o_ref.dtype)
        return carry
    lax.fori_loop(0, a_len // FFT_GROUP, epilogue, 0)


def _lconv(u_arr, u_col, gate_arr, gate_col, fa, fai, gc, gci, kr, ki, bias, order_idx, width, a_len, kp, tc=256):
    B, L, _ = u_arr.shape
    nh = width // tc
    const2 = lambda h, b: (0, 0)
    const3 = lambda h, b: (0, 0, 0)
    spec = lambda h, b: (0, 0, order_idx * nh + h)
    return pl.pallas_call(
        functools.partial(_lconv_kernel, a_len=a_len, kp=kp),
        grid=(nh, B),
        in_specs=[pl.BlockSpec((1, L, tc), lambda h, b: (b, 0, u_col * nh + h)),
                  pl.BlockSpec((1, L, tc), lambda h, b: (b, 0, gate_col * nh + h)),
                  pl.BlockSpec(fa.shape, const2), pl.BlockSpec(fai.shape, const2),
                  pl.BlockSpec(gc.shape, const3, pipeline_mode=pl.Buffered(1)),
                  pl.BlockSpec(gci.shape, const3, pipeline_mode=pl.Buffered(1)),
                  pl.BlockSpec((kp, FFT_NB, tc), spec, pipeline_mode=pl.Buffered(1)),
                  pl.BlockSpec((kp, FFT_NB, tc), spec, pipeline_mode=pl.Buffered(1)),
                  pl.BlockSpec((1, tc), lambda h, b: (0, h))],
        out_specs=pl.BlockSpec((1, L, tc), lambda h, b: (b, 0, h)),
        out_shape=jax.ShapeDtypeStruct((B, L, width), BF16),
        scratch_shapes=[pltpu.VMEM((tc // V7X_LANES, a_len * _pitch(FFT_NB), V7X_LANES), F32),
                        pltpu.VMEM((tc // V7X_LANES, FFT_NB * _pitch(2 * kp), V7X_LANES), F32)],
        compiler_params=_cparams(2, VMEM_LIMIT),
        name="lconv%d" % order_idx,
    )(u_arr, gate_arr, fa, fai, gc, gci, kr, ki, bias)


def _merge_kernel(x_ref, yhy_ref, oat_ref, g_ref, g1_ref, lng_ref, lnb_ref, l1g_ref, l1b_ref,
                  whyo_ref, wato_ref, wout_ref, o_ref, *, alpha):
    d = whyo_ref.shape[1]
    xln = _layer_norm(x_ref[0], lng_ref[...], lnb_ref[...])
    g = g_ref[0].astype(F32)
    m = (_sigmoid(g[:, :d]) * _dot(yhy_ref[0], whyo_ref[...])
         + _sigmoid(g[:, d:]) * _dot(oat_ref[0], wato_ref[...]))
    y = _dot(m.astype(BF16), wout_ref[...])
    o_ref[0] = _layer_norm(alpha * xln + g1_ref[0] * y, l1g_ref[...], l1b_ref[...])


def _merge(x, yhy, oat, g, g1, lng, lnb, l1g, l1b, whyo, wato, wout, alpha, tm=512):
    B, S, D = x.shape
    row = lambda b, i: (b, i, 0)
    mod = lambda b, i: (b, 0, 0)
    const = lambda b, i: (0, 0)
    vec = pl.BlockSpec((1, D), const)
    return pl.pallas_call(
        functools.partial(_merge_kernel, alpha=alpha),
        grid=(B, S // tm),
        in_specs=[pl.BlockSpec((1, tm, D), row),
                  pl.BlockSpec((1, tm, yhy.shape[2]), row), pl.BlockSpec((1, tm, oat.shape[2]), row),
                  pl.BlockSpec((1, tm, g.shape[2]), row),
                  pl.BlockSpec((1, 1, D), mod), vec, vec, vec, vec,
                  pl.BlockSpec(whyo.shape, const), pl.BlockSpec(wato.shape, const), pl.BlockSpec(wout.shape, const)],
        out_specs=pl.BlockSpec((1, tm, D), row),
        out_shape=jax.ShapeDtypeStruct((B, S, D), F32),
        compiler_params=_cparams(2, VMEM_LIMIT),
        name="merge",
    )(x, yhy, oat, g, g1, lng, lnb, l1g, l1b, whyo, wato, wout)


def _ffn_kernel(x_ref, xp_ref, xn_ref, sh_ref, sc_ref, g2_ref, wup_ref, cw_ref, cb_ref, wdn_ref,
                l2g_ref, l2b_ref, o_ref, *, alpha, d_ff, tc, halo):
    i = pl.program_id(1)
    n_i = pl.num_programs(1)
    x = x_ref[0]
    tm = x.shape[0]
    sc = 1.0 + sc_ref[0]
    sh = sh_ref[0]
    hp = jnp.where(i > 0, xp_ref[0] * sc + sh, 0.0)
    hn = jnp.where(i < n_i - 1, xn_ref[0] * sc + sh, 0.0)
    h = jnp.concatenate([hp, x * sc + sh, hn], axis=0).astype(BF16)
    rows = tm + 2 * halo

    def conv(u, lo, sz):
        w = cw_ref[:, lo:lo + sz]
        um = pltpu.roll(u, 1, 0)[halo:halo + tm]
        up = pltpu.roll(u, rows - 1, 0)[halo:halo + tm]
        return um * w[0:1] + u[halo:halo + tm] * w[1:2] + up * w[2:3] + cb_ref[:, lo:lo + sz]

    def up(lo, sz):
        return _dot(h, wup_ref[:, lo:lo + sz]), _dot(h, wup_ref[:, d_ff + lo:d_ff + lo + sz])

    chunks = [(lo, min(tc, d_ff - lo)) for lo in range(0, d_ff, tc)]
    acc = jnp.zeros((tm, o_ref.shape[2]), F32)
    cur = up(*chunks[0])
    for c, (lo, sz) in enumerate(chunks):
        nxt = up(*chunks[c + 1]) if c + 1 < len(chunks) else None
        a = conv(cur[0], lo, sz)
        g = conv(cur[1], d_ff + lo, sz)
        act = (g * _sigmoid(g) * a).astype(BF16)
        acc = acc + _dot(act, wdn_ref[lo:lo + sz, :])
        cur = nxt
    o_ref[0] = _layer_norm(alpha * x + g2_ref[0] * acc, l2g_ref[...], l2b_ref[...])


def _ffn(x1, sh, sc, g2, wup, cw, cb, wdn, l2g, l2b, alpha, tm=512, tc=512):
    B, S, D = x1.shape
    d_ff = wdn.shape[0]
    halo = V7X_SUBLANES
    nb = tm // halo
    row = lambda b, i: (b, i, 0)
    prev = lambda b, i: (b, jnp.maximum(i * nb - 1, 0), 0)
    nxt = lambda b, i: (b, jnp.minimum((i + 1) * nb, S // halo - 1), 0)
    mod = lambda b, i: (b, 0, 0)
    const = lambda b, i: (0, 0)
    vec = pl.BlockSpec((1, D), const)
    return pl.pallas_call(
        functools.partial(_ffn_kernel, alpha=alpha, d_ff=d_ff, tc=tc, halo=halo),
        grid=(B, S // tm),
        in_specs=[pl.BlockSpec((1, tm, D), row), pl.BlockSpec((1, halo, D), prev), pl.BlockSpec((1, halo, D), nxt),
                  pl.BlockSpec((1, 1, D), mod), pl.BlockSpec((1, 1, D), mod), pl.BlockSpec((1, 1, D), mod),
                  pl.BlockSpec(wup.shape, const, pipeline_mode=pl.Buffered(1)),
                  pl.BlockSpec(cw.shape, const), pl.BlockSpec(cb.shape, const),
                  pl.BlockSpec(wdn.shape, const, pipeline_mode=pl.Buffered(1)),
                  vec, vec],
        out_specs=pl.BlockSpec((1, tm, D), row),
        out_shape=jax.ShapeDtypeStruct((B, S, D), F32),
        compiler_params=_cparams(2, VMEM_LIMIT),
        name="ffn",
    )(x1, x1, x1, sh, sc, g2, wup, cw, cb, wdn, l2g, l2b)


def _rope_tables(S, at_d):
    rope_axis = at_d // 2
    half = rope_axis // 2
    pos = np.arange(S)
    rowp = (pos // GRID_W).astype(np.float64)
    colp = (pos % GRID_W).astype(np.float64)
    inv = ROPE_BASE ** (-np.arange(0, rope_axis, 2, dtype=np.float64) / rope_axis)
    lane = np.arange(V7X_LANES)
    d = lane % at_d
    e = d % rope_axis
    f = e % half
    ang = np.where((d < rope_axis)[None, :], rowp[:, None], colp[:, None]) * inv[f][None, :]
    first = (e < half)[None, :]
    sin = np.sin(ang)
    as_f32 = lambda m: jnp.asarray(m, dtype=F32)
    return as_f32(np.cos(ang)), as_f32(np.where(first, -sin, 0.0)), as_f32(np.where(first, 0.0, sin))


def _filter_tables(L, emb, width):
    bands = (emb - 1) // 2
    t = np.linspace(0.0, 1.0, L)[:, None]
    w = (2.0 * np.pi / L) * np.arange(L, dtype=np.float64)[:, None]
    f = np.linspace(1e-4, bands - 1, bands)[None, :]
    z = np.concatenate([t, np.cos(f * w), -np.sin(f * w)], -1)
    min_decay = math.log(HY_DECAY_TARGET) / HY_SLOW_PCT
    max_decay = math.log(HY_DECAY_TARGET) / HY_FAST_PCT
    deltas = np.linspace(min_decay, max_decay, width)
    decay = np.exp(-t * np.abs(deltas)[None, :])
    epad = (-emb) % V7X_LANES
    z = np.pad(z, ((0, 0), (0, epad)))
    return jnp.asarray(z, dtype=F32), jnp.asarray(decay, dtype=F32)


def _fft_tables(L):
    a_len = L // FFT_NB
    assert a_len * FFT_NB == L and a_len % FFT_GROUP == 0
    n_fft, n1 = 2 * L, 2 * a_len
    kp = -(-(a_len + 1) // FFT_GROUP) * FFT_GROUP
    k1 = np.arange(kp)
    valid = (k1 <= a_len).astype(np.float64)
    ph = (np.outer(k1, np.arange(a_len)) % n1) * (2.0 * np.pi / n1)
    ca, sa = np.cos(ph) * valid[:, None], np.sin(ph) * valid[:, None]
    fa = np.concatenate([ca, -sa], axis=0)
    w = np.where((k1 == 0) | (k1 == a_len), 1.0, 2.0) / n_fft
    fai = np.concatenate([(ca * w[:, None]).T, (-sa * w[:, None]).T], axis=1)
    r = np.arange(FFT_NB)
    kk = k1[:, None, None] + n1 * r[None, :, None]
    th = ((kk * r[None, None, :]) % n_fft) * (2.0 * np.pi / n_fft)
    gr, gi = np.cos(th) * valid[:, None, None], -np.sin(th) * valid[:, None, None]
    gc = np.concatenate([np.concatenate([gr, -gi], 2), np.concatenate([gi, gr], 2)], 1)
    grt, git = gr.transpose(0, 2, 1), gi.transpose(0, 2, 1)
    gci = np.concatenate([np.concatenate([grt, git], 2), np.concatenate([-git, grt], 2)], 1)
    as_bf16 = lambda m: jnp.asarray(m, dtype=F32).astype(BF16)
    return a_len, kp, as_bf16(fa), as_bf16(fai), as_bf16(gc), as_bf16(gci)


def kernel(x, c, ctx, c_ctx, ln_in_g, ln_in_b, w_ada, b_ada, w_in, hy_conv_w, hy_conv_b, hy_f_w1, hy_f_b1, hy_f_w2, hy_f_b2, hy_f_w3, hy_f_b3, hy_f_freq, hy_f_wout, hy_bias, lam_q1, lam_k1, lam_q2, lam_k2, at_subln_g, w_hy_o, w_at_o, w_out, ln1_g, ln1_b, ffn_w_up, ffn_conv_w, ffn_conv_b, ffn_w_down, ln2_g, ln2_b):
    B, S, D = x.shape
    depth = w_ada.shape[0]
    assert depth == 1, "single-layer configuration only"
    l = 0
    order, width = hy_bias.shape[1], hy_bias.shape[2]
    at_d = lam_q1.shape[1]
    at_w = w_at_o.shape[1]
    hy_cols = (order + 1) * width
    emb = hy_f_w1.shape[1]
    alpha = (2.0 * depth) ** 0.25
    lam_init = 0.8 - 0.6 * math.exp(-0.3 * l)
    assert 2 * at_d == V7X_LANES and order == 2

    row2 = lambda a: a.reshape(1, -1)

    pad = (-(B + 1)) % V7X_SUBLANES
    cc = jnp.concatenate([c, c_ctx[None, :], jnp.zeros((pad, D), F32)], 0)
    mod = _ada(cc, w_ada[l], row2(b_ada[l]))
    sh1, sc1, g1, sh2, sc2, g2 = [mod[:B, i * D:(i + 1) * D].reshape(B, 1, D) for i in range(N_ADA)]
    sh1c, sc1c = mod[B:B + 1, 0:D], mod[B:B + 1, D:2 * D]

    lng, lnb = row2(ln_in_g), row2(ln_in_b)
    w_in_b = w_in[l].astype(BF16)
    cos, sin_a, sin_b = _rope_tables(S, at_d)
    u3, qt, k_l, vt_l, gates = _inproj(x, sh1, sc1, lng, lnb, w_in_b, hy_conv_w[l], row2(hy_conv_b[l]),
                                       cos, sin_a, sin_b, hy_cols, at_w, at_d)
    k_c, vt_c = _ctx_kv(ctx, sh1c, sc1c, lng, lnb, w_in_b[:, hy_cols + at_w:hy_cols + 3 * at_w], at_w)
    o_at = _attn(qt, k_l, vt_l, k_c, vt_c, row2(lam_q1[l]), row2(lam_k1[l]), row2(lam_q2[l]), row2(lam_k2[l]),
                 row2(at_subln_g[l]), lam_init, at_d)

    zemb, decay = _filter_tables(S, emb, width)
    w1p = jnp.pad(hy_f_w1[l], ((0, zemb.shape[1] - emb), (0, 0)))
    f_taps, b_taps = _filt(zemb, w1p, row2(hy_f_b1[l]), hy_f_w2[l], row2(hy_f_b2[l]), hy_f_w3[l],
                           row2(hy_f_b3[l]), hy_f_freq[l], hy_f_wout[l], decay, width, order)
    a_len, kp, fa, fai, gc, gci = _fft_tables(S)
    kr, ki = _fspec(f_taps, b_taps, fa, gc, a_len, kp)
    zz = _lconv(u3, 0, u3, 1, fa, fai, gc, gci, kr, ki, row2(hy_bias[l, 0]), 0, width, a_len, kp)
    y_hy = _lconv(zz, 0, u3, 2, fa, fai, gc, gci, kr, ki, row2(hy_bias[l, 1]), 1, width, a_len, kp)

    x1 = _merge(x, y_hy, o_at, gates, g1, lng, lnb, row2(ln1_g[l]), row2(ln1_b[l]),
                w_hy_o[l].astype(BF16), w_at_o[l].astype(BF16), w_out[l].astype(BF16), alpha)
    return _ffn(x1, sh2, sc2, g2, ffn_w_up[l].astype(BF16), ffn_conv_w[l], row2(ffn_conv_b[l]),
                ffn_w_down[l].astype(BF16), row2(ln2_g[l]), row2(ln2_b[l]), alpha)
```

```python
import functools
import math

import numpy as np
import jax
import jax.numpy as jnp
from jax import lax
from jax.experimental import pallas as pl
from jax.experimental.pallas import tpu as pltpu

F32 = jnp.float32
BF16 = jnp.bfloat16

LN_EPS = 1e-5
RMS_EPS = 1e-5
GRID_W = 64
ROPE_BASE = 10000.0
N_ADA = 6
HY_DECAY_TARGET = 1e-2
HY_FAST_PCT = 0.3
HY_SLOW_PCT = 1.5

V7X_LANES = 128
V7X_SUBLANES = 8
V7X_VMEM_BYTES = 64 * 1024 * 1024
VMEM_LIMIT = 56 * 1024 * 1024

HIGHEST = lax.Precision.HIGHEST


def _cparams(n_axes, vmem=None):
    return pltpu.CompilerParams(dimension_semantics=("arbitrary",) * n_axes, vmem_limit_bytes=vmem)


def _layer_norm(x, g, b):
    mu = jnp.mean(x, axis=-1, keepdims=True)
    xc = x - mu
    var = jnp.mean(xc * xc, axis=-1, keepdims=True)
    return xc * lax.rsqrt(var + LN_EPS) * g + b


def _sigmoid(x):
    return 1.0 / (1.0 + jnp.exp(-x))


def _dot(a, b):
    return jnp.dot(a, b, preferred_element_type=F32)


def _ada_kernel(c_ref, w_ref, b_ref, o_ref):
    c = c_ref[...]
    s = c * _sigmoid(c)
    o_ref[...] = jnp.dot(s, w_ref[...], preferred_element_type=F32, precision=HIGHEST) + b_ref[...]


def _ada(cc, w, b, tn=1536):
    rows, d = cc.shape
    n = w.shape[1]
    return pl.pallas_call(
        _ada_kernel,
        grid=(n // tn,),
        in_specs=[pl.BlockSpec((rows, d), lambda j: (0, 0)),
                  pl.BlockSpec((d, tn), lambda j: (0, j)),
                  pl.BlockSpec((1, tn), lambda j: (0, j))],
        out_specs=pl.BlockSpec((rows, tn), lambda j: (0, j)),
        out_shape=jax.ShapeDtypeStruct((rows, n), F32),
        compiler_params=_cparams(1, VMEM_LIMIT),
        name="ada",
    )(cc, w, b)


def _rope(x, cos, sin_a, sin_b, n_heads):
    outs = []
    for h in range(n_heads):
        xh = x[:, h * V7X_LANES:(h + 1) * V7X_LANES]
        up = pltpu.roll(xh, V7X_LANES - 16, 1)
        dn = pltpu.roll(xh, 16, 1)
        outs.append(xh * cos + up * sin_a + dn * sin_b)
    return jnp.concatenate(outs, axis=1)


def _inproj_kernel(x_ref, xp_ref, xn_ref, sh_ref, sc_ref, lng_ref, lnb_ref, w_ref, cw_ref, cb_ref,
                   cos_ref, sa_ref, sb_ref, hy_ref, q_ref, k_ref, v_ref, g_ref, *, hy_cols, at_w, q_scale, halo, tc):
    i = pl.program_id(0)
    n_i = pl.num_programs(0)
    lng, lnb, sc, sh = lng_ref[...], lnb_ref[...], 1.0 + sc_ref[0], sh_ref[0]
    mod = lambda xr: _layer_norm(xr, lng, lnb) * sc + sh
    h = mod(x_ref[0]).astype(BF16)
    tm = h.shape[0]
    hp = jnp.where(i > 0, mod(xp_ref[0]), 0.0).astype(BF16)
    hn = jnp.where(i < n_i - 1, mod(xn_ref[0]), 0.0).astype(BF16)
    h_cat = jnp.concatenate([hp, h, hn], axis=0)
    rows = tm + 2 * halo
    for lo in range(0, hy_cols, tc):
        z = _dot(h_cat, w_ref[:, lo:lo + tc])
        w = cw_ref[:, lo:lo + tc]
        zm = pltpu.roll(z, 1, 0)[halo:halo + tm]
        zp = pltpu.roll(z, rows - 1, 0)[halo:halo + tm]
        u = zm * w[0:1] + z[halo:halo + tm] * w[1:2] + zp * w[2:3] + cb_ref[:, lo:lo + tc]
        hy_ref[0, :, lo:lo + tc] = u.astype(hy_ref.dtype)
    n_heads = at_w // V7X_LANES
    o1 = hy_cols
    o2, o3, o4 = o1 + at_w, o1 + 2 * at_w, o1 + 3 * at_w
    cos, sa, sb = cos_ref[...], sa_ref[...], sb_ref[...]
    q = _rope(_dot(h, w_ref[:, o1:o2]), cos, sa, sb, n_heads)
    q_ref[0] = (q * q_scale).T.astype(q_ref.dtype)
    k = _rope(_dot(h, w_ref[:, o2:o3]), cos, sa, sb, n_heads)
    k_ref[0] = k.astype(k_ref.dtype)
    v_ref[0] = _dot(h, w_ref[:, o3:o4]).T.astype(v_ref.dtype)
    g_ref[0] = _dot(h, w_ref[:, o4:]).astype(g_ref.dtype)


def _inproj(x, sh, sc, lng, lnb, w, cw, cb, cos, sa, sb, hy_cols, at_w, at_d, tm=512):
    B, S, D = x.shape
    ncols = w.shape[1]
    g_cols = ncols - hy_cols - 3 * at_w
    halo = 2 * V7X_SUBLANES
    nb = tm // halo
    kern = functools.partial(_inproj_kernel, hy_cols=hy_cols, at_w=at_w, q_scale=at_d ** -0.5 * math.log2(math.e),
                             halo=halo, tc=512)
    row = lambda i, b: (b, i, 0)
    prev = lambda i, b: (b, jnp.maximum(i * nb - 1, 0), 0)
    nxt = lambda i, b: (b, jnp.minimum((i + 1) * nb, S // halo - 1), 0)
    rowt = lambda i, b: (b, 0, i)
    mod = lambda i, b: (b, 0, 0)
    const = lambda i, b: (0, 0)
    tab = lambda i, b: (i, 0)
    return pl.pallas_call(
        kern,
        grid=(S // tm, B),
        in_specs=[pl.BlockSpec((1, tm, D), row), pl.BlockSpec((1, halo, D), prev), pl.BlockSpec((1, halo, D), nxt),
                  pl.BlockSpec((1, 1, D), mod), pl.BlockSpec((1, 1, D), mod),
                  pl.BlockSpec((1, D), const), pl.BlockSpec((1, D), const),
                  pl.BlockSpec((D, ncols), const, pipeline_mode=pl.Buffered(1)),
                  pl.BlockSpec(cw.shape, const), pl.BlockSpec(cb.shape, const),
                  pl.BlockSpec((tm, V7X_LANES), tab), pl.BlockSpec((tm, V7X_LANES), tab),
                  pl.BlockSpec((tm, V7X_LANES), tab)],
        out_specs=[pl.BlockSpec((1, tm, hy_cols), row), pl.BlockSpec((1, at_w, tm), rowt),
                   pl.BlockSpec((1, tm, at_w), row), pl.BlockSpec((1, at_w, tm), rowt),
                   pl.BlockSpec((1, tm, g_cols), row)],
        out_shape=[jax.ShapeDtypeStruct((B, S, hy_cols), BF16), jax.ShapeDtypeStruct((B, at_w, S), BF16),
                   jax.ShapeDtypeStruct((B, S, at_w), BF16), jax.ShapeDtypeStruct((B, at_w, S), BF16),
                   jax.ShapeDtypeStruct((B, S, g_cols), BF16)],
        compiler_params=_cparams(2, VMEM_LIMIT),
        name="inproj",
    )(x, x, x, sh, sc, lng, lnb, w, cw, cb, cos, sa, sb)


def _ctx_kv_kernel(x_ref, sh_ref, sc_ref, lng_ref, lnb_ref, w_ref, k_ref, v_ref, *, at_w):
    xn = _layer_norm(x_ref[0], lng_ref[...], lnb_ref[...])
    h = (xn * (1.0 + sc_ref[...]) + sh_ref[...]).astype(BF16)
    k_ref[0] = _dot(h, w_ref[:, 0:at_w]).astype(k_ref.dtype)
    v_ref[0] = _dot(h, w_ref[:, at_w:]).T.astype(v_ref.dtype)


def _ctx_kv(ctx, sh, sc, lng, lnb, w_kv, at_w):
    B, C, D = ctx.shape
    const = lambda b: (0, 0)
    row = lambda b: (b, 0, 0)
    return pl.pallas_call(
        functools.partial(_ctx_kv_kernel, at_w=at_w),
        grid=(B,),
        in_specs=[pl.BlockSpec((1, C, D), row),
                  pl.BlockSpec((1, D), const), pl.BlockSpec((1, D), const),
                  pl.BlockSpec((1, D), const), pl.BlockSpec((1, D), const),
                  pl.BlockSpec((D, 2 * at_w), const)],
        out_specs=[pl.BlockSpec((1, C, at_w), row), pl.BlockSpec((1, at_w, C), row)],
        out_shape=[jax.ShapeDtypeStruct((B, C, at_w), BF16), jax.ShapeDtypeStruct((B, at_w, C), BF16)],
        compiler_params=_cparams(1),
        name="ctx_kv",
    )(ctx, sh, sc, lng, lnb, w_kv)


ATTN_AHEAD = 2


def _attn_kernel(qt_ref, k_ref, vt_ref, kc_ref, vtc_ref, lq1_ref, lk1_ref, lq2_ref, lk2_ref, g_ref, o_ref, *,
                 lam_init, at_d, kc):
    qt = qt_ref[0]
    hw, tq = qt.shape
    lam = (jnp.exp(jnp.sum(lq1_ref[...] * lk1_ref[...], axis=-1, keepdims=True))
           - jnp.exp(jnp.sum(lq2_ref[...] * lk2_ref[...], axis=-1, keepdims=True)) + lam_init)
    row = lax.broadcasted_iota(jnp.int32, qt.shape, 0)
    zero = jnp.zeros_like(qt)
    qts = (jnp.where(row < at_d, qt, zero), jnp.where(row >= at_d, qt, zero))
    state = [(jnp.full((1, tq), -1e30, F32), jnp.zeros((1, tq), F32), jnp.zeros((hw, tq), F32))
             for _ in range(2)]
    chunks = [(k_ref, vt_ref, off, kc) for off in range(0, k_ref.shape[1], kc)]
    chunks.append((kc_ref, vtc_ref, 0, kc_ref.shape[1]))
    units = [chunk + (comp,) for chunk in chunks for comp in range(2)]

    def scores(unit):
        kref, _, off, size, comp = unit
        return _dot(kref[0, off:off + size, :], qts[comp])

    pending = [scores(u) for u in units[:ATTN_AHEAD]]
    for i, (_, vref, off, size, comp) in enumerate(units):
        if i + ATTN_AHEAD < len(units):
            pending.append(scores(units[i + ATTN_AHEAD]))
        st = pending.pop(0)
        m, l, acc = state[comp]
        m_new = jnp.maximum(m, jnp.max(st, axis=0, keepdims=True))
        alpha = jnp.exp2(m - m_new)
        p = jnp.exp2(st - m_new)
        l = alpha * l + jnp.sum(p, axis=0, keepdims=True)
        acc = alpha * acc + _dot(vref[0, :, off:off + size], p.astype(BF16))
        state[comp] = (m_new, l, acc)
    (_, l1, a1), (_, l2, a2) = state
    ot = a1 * (1.0 / l1) - a2 * (lam / l2)
    ot = ot * lax.rsqrt(jnp.mean(ot * ot, axis=0, keepdims=True) + RMS_EPS)
    o_ref[0] = (ot.T * g_ref[...] * (1.0 - lam_init)).astype(o_ref.dtype)


def _attn(qt, k_lat, vt_lat, k_ctx, vt_ctx, lq1, lk1, lq2, lk2, subln_g, lam_init, at_d, tq=512, kc=1024):
    B, W, S = qt.shape
    n_ctx = k_ctx.shape[1]
    hw = 2 * at_d
    n_heads = W // hw
    assert S % kc == 0
    qmap = lambda b, h, i: (b, h, i)
    omap = lambda b, h, i: (b, i, h)
    kmap = lambda b, h, i: (b, 0, h)
    vmap = lambda b, h, i: (b, h, 0)
    const = lambda b, h, i: (0, 0)
    return pl.pallas_call(
        functools.partial(_attn_kernel, lam_init=lam_init, at_d=at_d, kc=kc),
        grid=(B, n_heads, S // tq),
        in_specs=[pl.BlockSpec((1, hw, tq), qmap),
                  pl.BlockSpec((1, S, hw), kmap), pl.BlockSpec((1, hw, S), vmap),
                  pl.BlockSpec((1, n_ctx, hw), kmap), pl.BlockSpec((1, hw, n_ctx), vmap),
                  pl.BlockSpec((1, at_d), const), pl.BlockSpec((1, at_d), const),
                  pl.BlockSpec((1, at_d), const), pl.BlockSpec((1, at_d), const),
                  pl.BlockSpec((1, hw), const)],
        out_specs=pl.BlockSpec((1, tq, hw), omap),
        out_shape=jax.ShapeDtypeStruct((B, S, W), BF16),
        compiler_params=_cparams(3, VMEM_LIMIT),
        name="attn",
    )(qt, k_lat, vt_lat, k_ctx, vt_ctx, lq1, lk1, lq2, lk2, subln_g)


def _filt_kernel(z_ref, w1_ref, b1_ref, w2_ref, b2_ref, w3_ref, b3_ref, fr_ref, wo_ref, dec_ref,
                 f_ref, b_ref, *, width, order):
    i = pl.program_id(0)
    hdot = lambda a, b: jnp.dot(a, b, preferred_element_type=F32, precision=HIGHEST)
    hdn = jnp.sin(fr_ref[0:1, :] * (hdot(z_ref[...], w1_ref[...]) + b1_ref[...]))
    hdn = jnp.sin(fr_ref[1:2, :] * (hdot(hdn, w2_ref[...]) + b2_ref[...]))
    hdn = jnp.sin(fr_ref[2:3, :] * (hdot(hdn, w3_ref[...]) + b3_ref[...]))
    h = _dot(hdn.astype(BF16), wo_ref[...].astype(BF16))
    dec = dec_ref[...]
    tl = dec.shape[0]
    row = lax.broadcasted_iota(jnp.int32, (tl, width), 0) + i * tl
    for o in range(order):
        f_ref[:, o * width:(o + 1) * width] = h[:, (2 * o) * width:(2 * o + 1) * width] * dec
        bwd = h[:, (2 * o + 1) * width:(2 * o + 2) * width] * dec
        b_ref[:, o * width:(o + 1) * width] = jnp.where(row == 0, 0.0, bwd)


def _filt(zemb, w1, b1, w2, b2, w3, b3, freq, wout, decay, width, order, tl=512):
    L, E = zemb.shape
    F = w2.shape[0]
    const = lambda i: (0, 0)
    rowm = lambda i: (i, 0)
    return pl.pallas_call(
        functools.partial(_filt_kernel, width=width, order=order),
        grid=(L // tl,),
        in_specs=[pl.BlockSpec((tl, E), rowm),
                  pl.BlockSpec((E, F), const), pl.BlockSpec((1, F), const),
                  pl.BlockSpec((F, F), const), pl.BlockSpec((1, F), const),
                  pl.BlockSpec((F, F), const), pl.BlockSpec((1, F), const),
                  pl.BlockSpec((3, F), const),
                  pl.BlockSpec((F, order * 2 * width), const),
                  pl.BlockSpec((tl, width), rowm)],
        out_specs=[pl.BlockSpec((tl, order * width), rowm), pl.BlockSpec((tl, order * width), rowm)],
        out_shape=[jax.ShapeDtypeStruct((L, order * width), F32), jax.ShapeDtypeStruct((L, order * width), F32)],
        compiler_params=_cparams(1, VMEM_LIMIT),
        name="filt",
    )(zemb, w1, b1, w2, b2, w3, b3, freq, wout, decay)


FFT_NB = 64
FFT_GROUP = 8
FFT_RGROUP = 16


def _pitch(rows):
    p = -(-rows // V7X_SUBLANES)
    return (p + 1 - p % 2) * V7X_SUBLANES


def _ld(ref, start, size, stride=None):
    idx = pl.ds(start, size) if stride is None else pl.ds(start, size, stride=stride)
    return jnp.concatenate([ref[t, idx, :] for t in range(ref.shape[0])], axis=1)


def _st(ref, start, size, val, stride=None):
    idx = pl.ds(start, size) if stride is None else pl.ds(start, size, stride=stride)
    for t in range(ref.shape[0]):
        ref[t, idx, :] = val[:, t * V7X_LANES:(t + 1) * V7X_LANES]


def _stage_rows(src, uf_ref, a_len):
    up = _pitch(FFT_NB)

    def body(a, carry):
        _st(uf_ref, pl.multiple_of(a * up, V7X_SUBLANES), FFT_NB, src(pl.multiple_of(a * FFT_NB, FFT_NB)))
        return carry
    lax.fori_loop(0, a_len, body, 0)


def _fft_stage_a(uf_ref, s_ref, fa, *, a_len, kp):
    up, sp = _pitch(FFT_NB), _pitch(2 * kp)

    def body(g, carry):
        for j in range(FFT_RGROUP):
            r = g * FFT_RGROUP + j
            ur = _ld(uf_ref, r, a_len, up).astype(BF16)
            _st(s_ref, pl.multiple_of(r * sp, V7X_SUBLANES), 2 * kp, _dot(fa, ur))
        return carry
    lax.fori_loop(0, FFT_NB // FFT_RGROUP, body, 0)


def _fft_stage_c(s_ref, gc_ref, k1, kp):
    sp = _pitch(2 * kp)
    sr = _ld(s_ref, k1, FFT_NB, sp)
    si = _ld(s_ref, kp + k1, FFT_NB, sp)
    x = _dot(gc_ref[k1], jnp.concatenate([sr, si], axis=0).astype(BF16))
    return x[:FFT_NB], x[FFT_NB:]


def _fspec_kernel(f_ref, b_ref, fa_ref, gc_ref, kr_ref, ki_ref, uf_ref, s_ref, *, a_len, kp):
    fa = fa_ref[...]

    def transform(src_ref, emit):
        _stage_rows(lambda r0: src_ref[pl.ds(r0, FFT_NB), :], uf_ref, a_len)
        _fft_stage_a(uf_ref, s_ref, fa, a_len=a_len, kp=kp)

        def body(g, carry):
            for j in range(FFT_GROUP):
                k1 = g * FFT_GROUP + j
                xr, xi = _fft_stage_c(s_ref, gc_ref, k1, kp)
                emit(k1, xr, xi)
            return carry
        lax.fori_loop(0, kp // FFT_GROUP, body, 0)

    def emit_fwd(k1, xr, xi):
        kr_ref[k1] = xr
        ki_ref[k1] = xi

    def emit_bwd(k1, xr, xi):
        kr_ref[k1] += xr
        ki_ref[k1] -= xi

    transform(f_ref, emit_fwd)
    transform(b_ref, emit_bwd)


def _fspec(f_taps, b_taps, fa, gc, a_len, kp, tc=256):
    L, W = f_taps.shape
    const2 = lambda j: (0, 0)
    const3 = lambda j: (0, 0, 0)
    col = lambda j: (0, j)
    return pl.pallas_call(
        functools.partial(_fspec_kernel, a_len=a_len, kp=kp),
        grid=(W // tc,),
        in_specs=[pl.BlockSpec((L, tc), col), pl.BlockSpec((L, tc), col),
                  pl.BlockSpec(fa.shape, const2), pl.BlockSpec(gc.shape, const3)],
        out_specs=[pl.BlockSpec((kp, FFT_NB, tc), lambda j: (0, 0, j)),
                   pl.BlockSpec((kp, FFT_NB, tc), lambda j: (0, 0, j))],
        out_shape=[jax.ShapeDtypeStruct((kp, FFT_NB, W), F32), jax.ShapeDtypeStruct((kp, FFT_NB, W), F32)],
        scratch_shapes=[pltpu.VMEM((tc // V7X_LANES, a_len * _pitch(FFT_NB), V7X_LANES), F32),
                        pltpu.VMEM((tc // V7X_LANES, FFT_NB * _pitch(2 * kp), V7X_LANES), F32)],
        compiler_params=_cparams(1, VMEM_LIMIT),
        name="fspec",
    )(f_taps, b_taps, fa, gc)


def _lconv_kernel(u_ref, gate_ref, fa_ref, fai_ref, gc_ref, gci_ref, kr_ref, ki_ref, bias_ref, o_ref,
                  uf_ref, s_ref, *, a_len, kp):
    up, sp = _pitch(FFT_NB), _pitch(2 * kp)
    _stage_rows(lambda r0: u_ref[0, pl.ds(r0, FFT_NB), :].astype(F32), uf_ref, a_len)
    _fft_stage_a(uf_ref, s_ref, fa_ref[...], a_len=a_len, kp=kp)

    def spectrum_product(g, carry):
        k1s = [g * FFT_GROUP + j for j in range(FFT_GROUP)]
        xs = [_fft_stage_c(s_ref, gc_ref, k1, kp) for k1 in k1s]
        outs = []
        for k1, (xr, xi) in zip(k1s, xs):
            kr, ki = kr_ref[k1], ki_ref[k1]
            y = jnp.concatenate([xr * kr - xi * ki, xr * ki + xi * kr], axis=0).astype(BF16)
            outs.append(_dot(gci_ref[k1], y))
        for k1, bc in zip(k1s, outs):
            _st(s_ref, k1, FFT_NB, bc[:FFT_NB], sp)
            _st(s_ref, kp + k1, FFT_NB, bc[FFT_NB:], sp)
        return carry
    lax.fori_loop(0, kp // FFT_GROUP, spectrum_product, 0)

    fai = fai_ref[...]

    def inverse_a(g, carry):
        offs = [pl.multiple_of((g * FFT_RGROUP + j) * sp, V7X_SUBLANES) for j in range(FFT_RGROUP)]
        slabs = [_ld(s_ref, off, 2 * kp).astype(BF16) for off in offs]
        for off, slab in zip(offs, slabs):
            _st(s_ref, off, a_len, _dot(fai, slab))
        return carry
    lax.fori_loop(0, FFT_NB // FFT_RGROUP, inverse_a, 0)

    bias = bias_ref[...]

    def epilogue(g, carry):
        for j in range(FFT_GROUP):
            a = g * FFT_GROUP + j
            r0 = pl.multiple_of(a * FFT_NB, FFT_NB)
            y = _ld(s_ref, a, FFT_NB, sp)
            uf = _ld(uf_ref, pl.multiple_of(a * up, V7X_SUBLANES), FFT_NB)
            gate = gate_ref[0, pl.ds(r0, FFT_NB), :].astype(F32)
            o_ref[0, pl.ds(r0, FFT_NB), :] = (gate * (y + uf * bias)).astype(o_ref.dtype)
        return carry
    lax.fori_loop(0, a_len // FFT_GROUP, epilogue, 0)


def _lconv(u_arr, u_col, gate_arr, gate_col, fa, fai, gc, gci, kr, ki, bias, order_idx, width, a_len, kp, tc=256):
    B, L, _ = u_arr.shape
    nh = width // tc
    const2 = lambda h, b: (0, 0)
    const3 = lambda h, b: (0, 0, 0)
    spec = lambda h, b: (0, 0, order_idx * nh + h)
    return pl.pallas_call(
        functools.partial(_lconv_kernel, a_len=a_len, kp=kp),
        grid=(nh, B),
        in_specs=[pl.BlockSpec((1, L, tc), lambda h, b: (b, 0, u_col * nh + h)),
                  pl.BlockSpec((1, L, tc), lambda h, b: (b, 0, gate_col * nh + h)),
                  pl.BlockSpec(fa.shape, const2), pl.BlockSpec(fai.shape, const2),
                  pl.BlockSpec(gc.shape, const3, pipeline_mode=pl.Buffered(1)),
                  pl.BlockSpec(gci.shape, const3, pipeline_mode=pl.Buffered(1)),
                  pl.BlockSpec((kp, FFT_NB, tc), spec, pipeline_mode=pl.Buffered(1)),
                  pl.BlockSpec((kp, FFT_NB, tc), spec, pipeline_mode=pl.Buffered(1)),
                  pl.BlockSpec((1, tc), lambda h, b: (0, h))],
        out_specs=pl.BlockSpec((1, L, tc), lambda h, b: (b, 0, h)),
        out_shape=jax.ShapeDtypeStruct((B, L, width), BF16),
        scratch_shapes=[pltpu.VMEM((tc // V7X_LANES, a_len * _pitch(FFT_NB), V7X_LANES), F32),
                        pltpu.VMEM((tc // V7X_LANES, FFT_NB * _pitch(2 * kp), V7X_LANES), F32)],
        compiler_params=_cparams(2, VMEM_LIMIT),
        name="lconv%d" % order_idx,
    )(u_arr, gate_arr, fa, fai, gc, gci, kr, ki, bias)


def _merge_kernel(x_ref, yhy_ref, oat_ref, g_ref, g1_ref, lng_ref, lnb_ref, l1g_ref, l1b_ref,
                  whyo_ref, wato_ref, wout_ref, o_ref, *, alpha):
    d = whyo_ref.shape[1]
    xln = _layer_norm(x_ref[0], lng_ref[...], lnb_ref[...])
    g = g_ref[0].astype(F32)
    m = (_sigmoid(g[:, :d]) * _dot(yhy_ref[0], whyo_ref[...])
         + _sigmoid(g[:, d:]) * _dot(oat_ref[0], wato_ref[...]))
    y = _dot(m.astype(BF16), wout_ref[...])
    o_ref[0] = _layer_norm(alpha * xln + g1_ref[0] * y, l1g_ref[...], l1b_ref[...])


def _merge(x, yhy, oat, g, g1, lng, lnb, l1g, l1b, whyo, wato, wout, alpha, tm=512):
    B, S, D = x.shape
    row = lambda b, i: (b, i, 0)
    mod = lambda b, i: (b, 0, 0)
    const = lambda b, i: (0, 0)
    vec = pl.BlockSpec((1, D), const)
    return pl.pallas_call(
        functools.partial(_merge_kernel, alpha=alpha),
        grid=(B, S // tm),
        in_specs=[pl.BlockSpec((1, tm, D), row),
                  pl.BlockSpec((1, tm, yhy.shape[2]), row), pl.BlockSpec((1, tm, oat.shape[2]), row),
                  pl.BlockSpec((1, tm, g.shape[2]), row),
                  pl.BlockSpec((1, 1, D), mod), vec, vec, vec, vec,
                  pl.BlockSpec(whyo.shape, const), pl.BlockSpec(wato.shape, const), pl.BlockSpec(wout.shape, const)],
        out_specs=pl.BlockSpec((1, tm, D), row),
        out_shape=jax.ShapeDtypeStruct((B, S, D), F32),
        compiler_params=_cparams(2, VMEM_LIMIT),
        name="merge",
    )(x, yhy, oat, g, g1, lng, lnb, l1g, l1b, whyo, wato, wout)


def _ffn_kernel(x_ref, xp_ref, xn_ref, sh_ref, sc_ref, g2_ref, wup_ref, cw_ref, cb_ref, wdn_ref,
                l2g_ref, l2b_ref, o_ref, *, alpha, d_ff, tc, halo):
    i = pl.program_id(1)
    n_i = pl.num_programs(1)
    x = x_ref[0]
    tm = x.shape[0]
    sc = 1.0 + sc_ref[0]
    sh = sh_ref[0]
    hp = jnp.where(i > 0, xp_ref[0] * sc + sh, 0.0)
    hn = jnp.where(i < n_i - 1, xn_ref[0] * sc + sh, 0.0)
    h = jnp.concatenate([hp, x * sc + sh, hn], axis=0).astype(BF16)
    rows = tm + 2 * halo

    def conv(u, lo, sz):
        w = cw_ref[:, lo:lo + sz]
        um = pltpu.roll(u, 1, 0)[halo:halo + tm]
        up = pltpu.roll(u, rows - 1, 0)[halo:halo + tm]
        return um * w[0:1] + u[halo:halo + tm] * w[1:2] + up * w[2:3] + cb_ref[:, lo:lo + sz]

    def up(lo, sz):
        return _dot(h, wup_ref[:, lo:lo + sz]), _dot(h, wup_ref[:, d_ff + lo:d_ff + lo + sz])

    chunks = [(lo, min(tc, d_ff - lo)) for lo in range(0, d_ff, tc)]
    cur = up(*chunks[0])
    acts = []
    for c, (lo, sz) in enumerate(chunks):
        nxt = up(*chunks[c + 1]) if c + 1 < len(chunks) else None
        a = conv(cur[0], lo, sz)
        g = conv(cur[1], d_ff + lo, sz)
        acts.append((g * _sigmoid(g) * a).astype(BF16))
        cur = nxt
    acc = _dot(jnp.concatenate(acts, axis=1), wdn_ref[...])
    o_ref[0] = _layer_norm(alpha * x + g2_ref[0] * acc, l2g_ref[...], l2b_ref[...])


def _ffn(x1, sh, sc, g2, wup, cw, cb, wdn, l2g, l2b, alpha, tm=512, tc=512):
    B, S, D = x1.shape
    d_ff = wdn.shape[0]
    halo = V7X_SUBLANES
    nb = tm // halo
    row = lambda b, i: (b, i, 0)
    prev = lambda b, i: (b, jnp.maximum(i * nb - 1, 0), 0)
    nxt = lambda b, i: (b, jnp.minimum((i + 1) * nb, S // halo - 1), 0)
    mod = lambda b, i: (b, 0, 0)
    const = lambda b, i: (0, 0)
    vec = pl.BlockSpec((1, D), const)
    return pl.pallas_call(
        functools.partial(_ffn_kernel, alpha=alpha, d_ff=d_ff, tc=tc, halo=halo),
        grid=(B, S // tm),
        in_specs=[pl.BlockSpec((1, tm, D), row), pl.BlockSpec((1, halo, D), prev), pl.BlockSpec((1, halo, D), nxt),
                  pl.BlockSpec((1, 1, D), mod), pl.BlockSpec((1, 1, D), mod), pl.BlockSpec((1, 1, D), mod),
                  pl.BlockSpec(wup.shape, const, pipeline_mode=pl.Buffered(1)),
                  pl.BlockSpec(cw.shape, const), pl.BlockSpec(cb.shape, const),
                  pl.BlockSpec(wdn.shape, const, pipeline_mode=pl.Buffered(1)),
                  vec, vec],
        out_specs=pl.BlockSpec((1, tm, D), row),
        out_shape=jax.ShapeDtypeStruct((B, S, D), F32),
        compiler_params=_cparams(2, VMEM_LIMIT),
        name="ffn",
    )(x1, x1, x1, sh, sc, g2, wup, cw, cb, wdn, l2g, l2b)


def _rope_tables(S, at_d):
    rope_axis = at_d // 2
    half = rope_axis // 2
    pos = np.arange(S)
    rowp = (pos // GRID_W).astype(np.float64)
    colp = (pos % GRID_W).astype(np.float64)
    inv = ROPE_BASE ** (-np.arange(0, rope_axis, 2, dtype=np.float64) / rope_axis)
    lane = np.arange(V7X_LANES)
    d = lane % at_d
    e = d % rope_axis
    f = e % half
    ang = np.where((d < rope_axis)[None, :], rowp[:, None], colp[:, None]) * inv[f][None, :]
    first = (e < half)[None, :]
    sin = np.sin(ang)
    as_f32 = lambda m: jnp.asarray(m, dtype=F32)
    return as_f32(np.cos(ang)), as_f32(np.where(first, -sin, 0.0)), as_f32(np.where(first, 0.0, sin))


def _filter_tables(L, emb, width):
    bands = (emb - 1) // 2
    t = np.linspace(0.0, 1.0, L)[:, None]
    w = (2.0 * np.pi / L) * np.arange(L, dtype=np.float64)[:, None]
    f = np.linspace(1e-4, bands - 1, bands)[None, :]
    z = np.concatenate([t, np.cos(f * w), -np.sin(f * w)], -1)
    min_decay = math.log(HY_DECAY_TARGET) / HY_SLOW_PCT
    max_decay = math.log(HY_DECAY_TARGET) / HY_FAST_PCT
    deltas = np.linspace(min_decay, max_decay, width)
    decay = np.exp(-t * np.abs(deltas)[None, :])
    epad = (-emb) % V7X_LANES
    z = np.pad(z, ((0, 0), (0, epad)))
    return jnp.asarray(z, dtype=F32), jnp.asarray(decay, dtype=F32)


def _fft_tables(L):
    a_len = L // FFT_NB
    assert a_len * FFT_NB == L and a_len % FFT_GROUP == 0
    n_fft, n1 = 2 * L, 2 * a_len
    kp = -(-(a_len + 1) // FFT_GROUP) * FFT_GROUP
    k1 = np.arange(kp)
    valid = (k1 <= a_len).astype(np.float64)
    ph = (np.outer(k1, np.arange(a_len)) % n1) * (2.0 * np.pi / n1)
    ca, sa = np.cos(ph) * valid[:, None], np.sin(ph) * valid[:, None]
    fa = np.concatenate([ca, -sa], axis=0)
    w = np.where((k1 == 0) | (k1 == a_len), 1.0, 2.0) / n_fft
    fai = np.concatenate([(ca * w[:, None]).T, (-sa * w[:, None]).T], axis=1)
    r = np.arange(FFT_NB)
    kk = k1[:, None, None] + n1 * r[None, :, None]
    th = ((kk * r[None, None, :]) % n_fft) * (2.0 * np.pi / n_fft)
    gr, gi = np.cos(th) * valid[:, None, None], -np.sin(th) * valid[:, None, None]
    gc = np.concatenate([np.concatenate([gr, -gi], 2), np.concatenate([gi, gr], 2)], 1)
    grt, git = gr.transpose(0, 2, 1), gi.transpose(0, 2, 1)
    gci = np.concatenate([np.concatenate([grt, git], 2), np.concatenate([-git, grt], 2)], 1)
    as_bf16 = lambda m: jnp.asarray(m, dtype=F32).astype(BF16)
    return a_len, kp, as_bf16(fa), as_bf16(fai), as_bf16(gc), as_bf16(gci)


def kernel(x, c, ctx, c_ctx, ln_in_g, ln_in_b, w_ada, b_ada, w_in, hy_conv_w, hy_conv_b, hy_f_w1, hy_f_b1, hy_f_w2, hy_f_b2, hy_f_w3, hy_f_b3, hy_f_freq, hy_f_wout, hy_bias, lam_q1, lam_k1, lam_q2, lam_k2, at_subln_g, w_hy_o, w_at_o, w_out, ln1_g, ln1_b, ffn_w_up, ffn_conv_w, ffn_conv_b, ffn_w_down, ln2_g, ln2_b):
    B, S, D = x.shape
    depth = w_ada.shape[0]
    assert depth == 1, "single-layer configuration only"
    l = 0
    order, width = hy_bias.shape[1], hy_bias.shape[2]
    at_d = lam_q1.shape[1]
    at_w = w_at_o.shape[1]
    hy_cols = (order + 1) * width
    emb = hy_f_w1.shape[1]
    alpha = (2.0 * depth) ** 0.25
    lam_init = 0.8 - 0.6 * math.exp(-0.3 * l)
    assert 2 * at_d == V7X_LANES and order == 2

    row2 = lambda a: a.reshape(1, -1)

    pad = (-(B + 1)) % V7X_SUBLANES
    cc = jnp.concatenate([c, c_ctx[None, :], jnp.zeros((pad, D), F32)], 0)
    mod = _ada(cc, w_ada[l], row2(b_ada[l]))
    sh1, sc1, g1, sh2, sc2, g2 = [mod[:B, i * D:(i + 1) * D].reshape(B, 1, D) for i in range(N_ADA)]
    sh1c, sc1c = mod[B:B + 1, 0:D], mod[B:B + 1, D:2 * D]

    lng, lnb = row2(ln_in_g), row2(ln_in_b)
    w_in_b = w_in[l].astype(BF16)
    cos, sin_a, sin_b = _rope_tables(S, at_d)
    u3, qt, k_l, vt_l, gates = _inproj(x, sh1, sc1, lng, lnb, w_in_b, hy_conv_w[l], row2(hy_conv_b[l]),
                                       cos, sin_a, sin_b, hy_cols, at_w, at_d)
    k_c, vt_c = _ctx_kv(ctx, sh1c, sc1c, lng, lnb, w_in_b[:, hy_cols + at_w:hy_cols + 3 * at_w], at_w)
    o_at = _attn(qt, k_l, vt_l, k_c, vt_c, row2(lam_q1[l]), row2(lam_k1[l]), row2(lam_q2[l]), row2(lam_k2[l]),
                 row2(at_subln_g[l]), lam_init, at_d)

    zemb, decay = _filter_tables(S, emb, width)
    w1p = jnp.pad(hy_f_w1[l], ((0, zemb.shape[1] - emb), (0, 0)))
    f_taps, b_taps = _filt(zemb, w1p, row2(hy_f_b1[l]), hy_f_w2[l], row2(hy_f_b2[l]), hy_f_w3[l],
                           row2(hy_f_b3[l]), hy_f_freq[l], hy_f_wout[l], decay, width, order)
    a_len, kp, fa, fai, gc, gci = _fft_tables(S)
    kr, ki = _fspec(f_taps, b_taps, fa, gc, a_len, kp)
    zz = _lconv(u3, 0, u3, 1, fa, fai, gc, gci, kr, ki, row2(hy_bias[l, 0]), 0, width, a_len, kp)
    y_hy = _lconv(zz, 0, u3, 2, fa, fai, gc, gci, kr, ki, row2(hy_bias[l, 1]), 1, width, a_len, kp)

    x1 = _merge(x, y_hy, o_at, gates, g1, lng, lnb, row2(ln1_g[l]), row2(ln1_b[l]),
                w_hy_o[l].astype(BF16), w_at_o[l].astype(BF16), w_out[l].astype(BF16), alpha)
    return _ffn(x1, sh2, sc2, g2, ffn_w_up[l].astype(BF16), ffn_conv_w[l], row2(ffn_conv_b[l]),
                ffn_w_down[l].astype(BF16), row2(ln2_g[l]), row2(ln2_b[l]), alpha)
```

```python
import functools
import math

import numpy as np
import jax
import jax.numpy as jnp
from jax import lax
from jax.experimental import pallas as pl
from jax.experimental.pallas import tpu as pltpu

F32 = jnp.float32
BF16 = jnp.bfloat16

LN_EPS = 1e-5
RMS_EPS = 1e-5
GRID_W = 64
ROPE_BASE = 10000.0
N_ADA = 6
HY_DECAY_TARGET = 1e-2
HY_FAST_PCT = 0.3
HY_SLOW_PCT = 1.5

V7X_LANES = 128
V7X_SUBLANES = 8
V7X_VMEM_BYTES = 64 * 1024 * 1024
VMEM_LIMIT = 56 * 1024 * 1024

HIGHEST = lax.Precision.HIGHEST


def _cparams(n_axes, vmem=None):
    return pltpu.CompilerParams(dimension_semantics=("arbitrary",) * n_axes, vmem_limit_bytes=vmem)


def _layer_norm(x, g, b):
    mu = jnp.mean(x, axis=-1, keepdims=True)
    xc = x - mu
    var = jnp.mean(xc * xc, axis=-1, keepdims=True)
    return xc * lax.rsqrt(var + LN_EPS) * g + b


def _sigmoid(x):
    return 1.0 / (1.0 + jnp.exp(-x))


def _dot(a, b):
    return jnp.dot(a, b, preferred_element_type=F32)


def _ada_kernel(c_ref, w_ref, b_ref, o_ref):
    c = c_ref[...]
    s = c * _sigmoid(c)
    o_ref[...] = jnp.dot(s, w_ref[...], preferred_element_type=F32, precision=HIGHEST) + b_ref[...]


def _ada(cc, w, b, tn=1536):
    rows, d = cc.shape
    n = w.shape[1]
    return pl.pallas_call(
        _ada_kernel,
        grid=(n // tn,),
        in_specs=[pl.BlockSpec((rows, d), lambda j: (0, 0)),
                  pl.BlockSpec((d, tn), lambda j: (0, j)),
                  pl.BlockSpec((1, tn), lambda j: (0, j))],
        out_specs=pl.BlockSpec((rows, tn), lambda j: (0, j)),
        out_shape=jax.ShapeDtypeStruct((rows, n), F32),
        compiler_params=_cparams(1, VMEM_LIMIT),
        name="ada",
    )(cc, w, b)


def _rope(x, cos, sin_a, sin_b, n_heads):
    outs = []
    for h in range(n_heads):
        xh = x[:, h * V7X_LANES:(h + 1) * V7X_LANES]
        up = pltpu.roll(xh, V7X_LANES - 16, 1)
        dn = pltpu.roll(xh, 16, 1)
        outs.append(xh * cos + up * sin_a + dn * sin_b)
    return jnp.concatenate(outs, axis=1)


def _inproj_kernel(x_ref, xp_ref, xn_ref, sh_ref, sc_ref, lng_ref, lnb_ref, w_ref, cw_ref, cb_ref,
                   cos_ref, sa_ref, sb_ref, hy_ref, q_ref, k_ref, v_ref, g_ref, *, hy_cols, at_w, q_scale, halo, tc):
    i = pl.program_id(0)
    n_i = pl.num_programs(0)
    lng, lnb, sc, sh = lng_ref[...], lnb_ref[...], 1.0 + sc_ref[0], sh_ref[0]
    mod = lambda xr: _layer_norm(xr, lng, lnb) * sc + sh
    h = mod(x_ref[0]).astype(BF16)
    tm = h.shape[0]
    hp = jnp.where(i > 0, mod(xp_ref[0]), 0.0).astype(BF16)
    hn = jnp.where(i < n_i - 1, mod(xn_ref[0]), 0.0).astype(BF16)
    h_cat = jnp.concatenate([hp, h, hn], axis=0)
    rows = tm + 2 * halo
    for lo in range(0, hy_cols, tc):
        z = _dot(h_cat, w_ref[:, lo:lo + tc])
        w = cw_ref[:, lo:lo + tc]
        zm = pltpu.roll(z, 1, 0)[halo:halo + tm]
        zp = pltpu.roll(z, rows - 1, 0)[halo:halo + tm]
        u = zm * w[0:1] + z[halo:halo + tm] * w[1:2] + zp * w[2:3] + cb_ref[:, lo:lo + tc]
        hy_ref[0, :, lo:lo + tc] = u.astype(hy_ref.dtype)
    n_heads = at_w // V7X_LANES
    o1 = hy_cols
    o2, o3, o4 = o1 + at_w, o1 + 2 * at_w, o1 + 3 * at_w
    cos, sa, sb = cos_ref[...], sa_ref[...], sb_ref[...]
    q = _rope(_dot(h, w_ref[:, o1:o2]), cos, sa, sb, n_heads)
    q_ref[0] = (q * q_scale).T.astype(q_ref.dtype)
    k = _rope(_dot(h, w_ref[:, o2:o3]), cos, sa, sb, n_heads)
    k_ref[0] = k.astype(k_ref.dtype)
    v_ref[0] = _dot(h, w_ref[:, o3:o4]).T.astype(v_ref.dtype)
    g_ref[0] = _dot(h, w_ref[:, o4:]).astype(g_ref.dtype)


def _inproj(x, sh, sc, lng, lnb, w, cw, cb, cos, sa, sb, hy_cols, at_w, at_d, tm=512):
    B, S, D = x.shape
    ncols = w.shape[1]
    g_cols = ncols - hy_cols - 3 * at_w
    halo = 2 * V7X_SUBLANES
    nb = tm // halo
    kern = functools.partial(_inproj_kernel, hy_cols=hy_cols, at_w=at_w, q_scale=at_d ** -0.5 * math.log2(math.e),
                             halo=halo, tc=512)
    row = lambda i, b: (b, i, 0)
    prev = lambda i, b: (b, jnp.maximum(i * nb - 1, 0), 0)
    nxt = lambda i, b: (b, jnp.minimum((i + 1) * nb, S // halo - 1), 0)
    rowt = lambda i, b: (b, 0, i)
    mod = lambda i, b: (b, 0, 0)
    const = lambda i, b: (0, 0)
    tab = lambda i, b: (i, 0)
    return pl.pallas_call(
        kern,
        grid=(S // tm, B),
        in_specs=[pl.BlockSpec((1, tm, D), row), pl.BlockSpec((1, halo, D), prev), pl.BlockSpec((1, halo, D), nxt),
                  pl.BlockSpec((1, 1, D), mod), pl.BlockSpec((1, 1, D), mod),
                  pl.BlockSpec((1, D), const), pl.BlockSpec((1, D), const),
                  pl.BlockSpec((D, ncols), const, pipeline_mode=pl.Buffered(1)),
                  pl.BlockSpec(cw.shape, const), pl.BlockSpec(cb.shape, const),
                  pl.BlockSpec((tm, V7X_LANES), tab), pl.BlockSpec((tm, V7X_LANES), tab),
                  pl.BlockSpec((tm, V7X_LANES), tab)],
        out_specs=[pl.BlockSpec((1, tm, hy_cols), row), pl.BlockSpec((1, at_w, tm), rowt),
                   pl.BlockSpec((1, tm, at_w), row), pl.BlockSpec((1, at_w, tm), rowt),
                   pl.BlockSpec((1, tm, g_cols), row)],
        out_shape=[jax.ShapeDtypeStruct((B, S, hy_cols), BF16), jax.ShapeDtypeStruct((B, at_w, S), BF16),
                   jax.ShapeDtypeStruct((B, S, at_w), BF16), jax.ShapeDtypeStruct((B, at_w, S), BF16),
                   jax.ShapeDtypeStruct((B, S, g_cols), BF16)],
        compiler_params=_cparams(2, VMEM_LIMIT),
        name="inproj",
    )(x, x, x, sh, sc, lng, lnb, w, cw, cb, cos, sa, sb)


def _ctx_kv_kernel(x_ref, sh_ref, sc_ref, lng_ref, lnb_ref, w_ref, k_ref, v_ref, *, at_w):
    xn = _layer_norm(x_ref[0], lng_ref[...], lnb_ref[...])
    h = (xn * (1.0 + sc_ref[...]) + sh_ref[...]).astype(BF16)
    k_ref[0] = _dot(h, w_ref[:, 0:at_w]).astype(k_ref.dtype)
    v_ref[0] = _dot(h, w_ref[:, at_w:]).T.astype(v_ref.dtype)


def _ctx_kv(ctx, sh, sc, lng, lnb, w_kv, at_w):
    B, C, D = ctx.shape
    const = lambda b: (0, 0)
    row = lambda b: (b, 0, 0)
    return pl.pallas_call(
        functools.partial(_ctx_kv_kernel, at_w=at_w),
        grid=(B,),
        in_specs=[pl.BlockSpec((1, C, D), row),
                  pl.BlockSpec((1, D), const), pl.BlockSpec((1, D), const),
                  pl.BlockSpec((1, D), const), pl.BlockSpec((1, D), const),
                  pl.BlockSpec((D, 2 * at_w), const)],
        out_specs=[pl.BlockSpec((1, C, at_w), row), pl.BlockSpec((1, at_w, C), row)],
        out_shape=[jax.ShapeDtypeStruct((B, C, at_w), BF16), jax.ShapeDtypeStruct((B, at_w, C), BF16)],
        compiler_params=_cparams(1),
        name="ctx_kv",
    )(ctx, sh, sc, lng, lnb, w_kv)


ATTN_AHEAD = 4
ATTN_TQ_SUB = 256


def _attn_kernel(qt_ref, k_ref, vt_ref, kc_ref, vtc_ref, lq1_ref, lk1_ref, lq2_ref, lk2_ref, g_ref, o_ref, *,
                 lam_init, at_d, kc):
    qt = qt_ref[0]
    hw, tq = qt.shape
    lam = (jnp.exp(jnp.sum(lq1_ref[...] * lk1_ref[...], axis=-1, keepdims=True))
           - jnp.exp(jnp.sum(lq2_ref[...] * lk2_ref[...], axis=-1, keepdims=True)) + lam_init)
    row = lax.broadcasted_iota(jnp.int32, qt.shape, 0)
    zero = jnp.zeros_like(qt)
    n_sub = tq // ATTN_TQ_SUB
    sub = lambda a, j: a[:, j * ATTN_TQ_SUB:(j + 1) * ATTN_TQ_SUB]
    qts = [[sub(jnp.where(row < at_d, qt, zero), j) for j in range(n_sub)],
           [sub(jnp.where(row >= at_d, qt, zero), j) for j in range(n_sub)]]
    init = (jnp.full((1, ATTN_TQ_SUB), -1e30, F32), jnp.zeros((1, ATTN_TQ_SUB), F32),
            jnp.zeros((hw, ATTN_TQ_SUB), F32))
    state = {(comp, j): init for comp in range(2) for j in range(n_sub)}
    chunks = [(k_ref, vt_ref, off, kc) for off in range(0, k_ref.shape[1], kc)]
    chunks.append((kc_ref, vtc_ref, 0, kc_ref.shape[1]))
    units = [chunk + (comp, j) for chunk in chunks for comp in range(2) for j in range(n_sub)]

    def scores(unit):
        kref, _, off, size, comp, j = unit
        return _dot(kref[0, off:off + size, :], qts[comp][j])

    pending = [scores(u) for u in units[:ATTN_AHEAD]]
    for i, (_, vref, off, size, comp, j) in enumerate(units):
        if i + ATTN_AHEAD < len(units):
            pending.append(scores(units[i + ATTN_AHEAD]))
        st = pending.pop(0)
        m, l, acc = state[(comp, j)]
        m_new = jnp.maximum(m, jnp.max(st, axis=0, keepdims=True))
        alpha = jnp.exp2(m - m_new)
        p = jnp.exp2(st - m_new)
        l = alpha * l + jnp.sum(p, axis=0, keepdims=True)
        acc = alpha * acc + _dot(vref[0, :, off:off + size], p.astype(BF16))
        state[(comp, j)] = (m_new, l, acc)
    gather = lambda comp, idx: jnp.concatenate([state[(comp, j)][idx] for j in range(n_sub)], axis=1)
    l1, a1, l2, a2 = gather(0, 1), gather(0, 2), gather(1, 1), gather(1, 2)
    ot = a1 * (1.0 / l1) - a2 * (lam / l2)
    ot = ot * lax.rsqrt(jnp.mean(ot * ot, axis=0, keepdims=True) + RMS_EPS)
    o_ref[0] = (ot.T * g_ref[...] * (1.0 - lam_init)).astype(o_ref.dtype)


def _attn(qt, k_lat, vt_lat, k_ctx, vt_ctx, lq1, lk1, lq2, lk2, subln_g, lam_init, at_d, tq=512, kc=1024):
    B, W, S = qt.shape
    n_ctx = k_ctx.shape[1]
    hw = 2 * at_d
    n_heads = W // hw
    assert S % kc == 0 and tq % ATTN_TQ_SUB == 0
    qmap = lambda b, h, i: (b, h, i)
    omap = lambda b, h, i: (b, i, h)
    kmap = lambda b, h, i: (b, 0, h)
    vmap = lambda b, h, i: (b, h, 0)
    const = lambda b, h, i: (0, 0)
    return pl.pallas_call(
        functools.partial(_attn_kernel, lam_init=lam_init, at_d=at_d, kc=kc),
        grid=(B, n_heads, S // tq),
        in_specs=[pl.BlockSpec((1, hw, tq), qmap),
                  pl.BlockSpec((1, S, hw), kmap), pl.BlockSpec((1, hw, S), vmap),
                  pl.BlockSpec((1, n_ctx, hw), kmap), pl.BlockSpec((1, hw, n_ctx), vmap),
                  pl.BlockSpec((1, at_d), const), pl.BlockSpec((1, at_d), const),
                  pl.BlockSpec((1, at_d), const), pl.BlockSpec((1, at_d), const),
                  pl.BlockSpec((1, hw), const)],
        out_specs=pl.BlockSpec((1, tq, hw), omap),
        out_shape=jax.ShapeDtypeStruct((B, S, W), BF16),
        compiler_params=_cparams(3, VMEM_LIMIT),
        name="attn",
    )(qt, k_lat, vt_lat, k_ctx, vt_ctx, lq1, lk1, lq2, lk2, subln_g)


def _filt_kernel(z_ref, w1_ref, b1_ref, w2_ref, b2_ref, w3_ref, b3_ref, fr_ref, wo_ref, dec_ref,
                 f_ref, b_ref, *, width, order):
    i = pl.program_id(0)
    hdot = lambda a, b: jnp.dot(a, b, preferred_element_type=F32, precision=HIGHEST)
    hdn = jnp.sin(fr_ref[0:1, :] * (hdot(z_ref[...], w1_ref[...]) + b1_ref[...]))
    hdn = jnp.sin(fr_ref[1:2, :] * (hdot(hdn, w2_ref[...]) + b2_ref[...]))
    hdn = jnp.sin(fr_ref[2:3, :] * (hdot(hdn, w3_ref[...]) + b3_ref[...]))
    h = _dot(hdn.astype(BF16), wo_ref[...].astype(BF16))
    dec = dec_ref[...]
    tl = dec.shape[0]
    row = lax.broadcasted_iota(jnp.int32, (tl, width), 0) + i * tl
    for o in range(order):
        f_ref[:, o * width:(o + 1) * width] = h[:, (2 * o) * width:(2 * o + 1) * width] * dec
        bwd = h[:, (2 * o + 1) * width:(2 * o + 2) * width] * dec
        b_ref[:, o * width:(o + 1) * width] = jnp.where(row == 0, 0.0, bwd)


def _filt(zemb, w1, b1, w2, b2, w3, b3, freq, wout, decay, width, order, tl=512):
    L, E = zemb.shape
    F = w2.shape[0]
    const = lambda i: (0, 0)
    rowm = lambda i: (i, 0)
    return pl.pallas_call(
        functools.partial(_filt_kernel, width=width, order=order),
        grid=(L // tl,),
        in_specs=[pl.BlockSpec((tl, E), rowm),
                  pl.BlockSpec((E, F), const), pl.BlockSpec((1, F), const),
                  pl.BlockSpec((F, F), const), pl.BlockSpec((1, F), const),
                  pl.BlockSpec((F, F), const), pl.BlockSpec((1, F), const),
                  pl.BlockSpec((3, F), const),
                  pl.BlockSpec((F, order * 2 * width), const),
                  pl.BlockSpec((tl, width), rowm)],
        out_specs=[pl.BlockSpec((tl, order * width), rowm), pl.BlockSpec((tl, order * width), rowm)],
        out_shape=[jax.ShapeDtypeStruct((L, order * width), F32), jax.ShapeDtypeStruct((L, order * width), F32)],
        compiler_params=_cparams(1, VMEM_LIMIT),
        name="filt",
    )(zemb, w1, b1, w2, b2, w3, b3, freq, wout, decay)


FFT_NB = 64
FFT_GROUP = 8
FFT_RGROUP = 16


def _pitch(rows):
    p = -(-rows // V7X_SUBLANES)
    return (p + 1 - p % 2) * V7X_SUBLANES


def _ld(ref, start, size, stride=None):
    idx = pl.ds(start, size) if stride is None else pl.ds(start, size, stride=stride)
    return jnp.concatenate([ref[t, idx, :] for t in range(ref.shape[0])], axis=1)


def _st(ref, start, size, val, stride=None):
    idx = pl.ds(start, size) if stride is None else pl.ds(start, size, stride=stride)
    for t in range(ref.shape[0]):
        ref[t, idx, :] = val[:, t * V7X_LANES:(t + 1) * V7X_LANES]


def _stage_rows(src, uf_ref, a_len):
    up = _pitch(FFT_NB)

    def body(a, carry):
        _st(uf_ref, pl.multiple_of(a * up, V7X_SUBLANES), FFT_NB, src(pl.multiple_of(a * FFT_NB, FFT_NB)))
        return carry
    lax.fori_loop(0, a_len, body, 0)


def _fft_stage_a(uf_ref, s_ref, fa, *, a_len, kp):
    up, sp = _pitch(FFT_NB), _pitch(2 * kp)

    def body(g, carry):
        for j in range(FFT_RGROUP):
            r = g * FFT_RGROUP + j
            ur = _ld(uf_ref, r, a_len, up).astype(BF16)
            _st(s_ref, pl.multiple_of(r * sp, V7X_SUBLANES), 2 * kp, _dot(fa, ur))
        return carry
    lax.fori_loop(0, FFT_NB // FFT_RGROUP, body, 0)


def _fft_stage_c(s_ref, gc_ref, k1, kp):
    sp = _pitch(2 * kp)
    sr = _ld(s_ref, k1, FFT_NB, sp)
    si = _ld(s_ref, kp + k1, FFT_NB, sp)
    x = _dot(gc_ref[k1], jnp.concatenate([sr, si], axis=0).astype(BF16))
    return x[:FFT_NB], x[FFT_NB:]


def _fspec_kernel(f_ref, b_ref, fa_ref, gc_ref, kr_ref, ki_ref, uf_ref, s_ref, *, a_len, kp):
    fa = fa_ref[...]

    def transform(src_ref, emit):
        _stage_rows(lambda r0: src_ref[pl.ds(r0, FFT_NB), :], uf_ref, a_len)
        _fft_stage_a(uf_ref, s_ref, fa, a_len=a_len, kp=kp)

        def body(g, carry):
            for j in range(FFT_GROUP):
                k1 = g * FFT_GROUP + j
                xr, xi = _fft_stage_c(s_ref, gc_ref, k1, kp)
                emit(k1, xr, xi)
            return carry
        lax.fori_loop(0, kp // FFT_GROUP, body, 0)

    def emit_fwd(k1, xr, xi):
        kr_ref[k1] = xr
        ki_ref[k1] = xi

    def emit_bwd(k1, xr, xi):
        kr_ref[k1] += xr
        ki_ref[k1] -= xi

    transform(f_ref, emit_fwd)
    transform(b_ref, emit_bwd)


def _fspec(f_taps, b_taps, fa, gc, a_len, kp, tc=256):
    L, W = f_taps.shape
    const2 = lambda j: (0, 0)
    const3 = lambda j: (0, 0, 0)
    col = lambda j: (0, j)
    return pl.pallas_call(
        functools.partial(_fspec_kernel, a_len=a_len, kp=kp),
        grid=(W // tc,),
        in_specs=[pl.BlockSpec((L, tc), col), pl.BlockSpec((L, tc), col),
                  pl.BlockSpec(fa.shape, const2), pl.BlockSpec(gc.shape, const3)],
        out_specs=[pl.BlockSpec((kp, FFT_NB, tc), lambda j: (0, 0, j)),
                   pl.BlockSpec((kp, FFT_NB, tc), lambda j: (0, 0, j))],
        out_shape=[jax.ShapeDtypeStruct((kp, FFT_NB, W), F32), jax.ShapeDtypeStruct((kp, FFT_NB, W), F32)],
        scratch_shapes=[pltpu.VMEM((tc // V7X_LANES, a_len * _pitch(FFT_NB), V7X_LANES), F32),
                        pltpu.VMEM((tc // V7X_LANES, FFT_NB * _pitch(2 * kp), V7X_LANES), F32)],
        compiler_params=_cparams(1, VMEM_LIMIT),
        name="fspec",
    )(f_taps, b_taps, fa, gc)


def _lconv_kernel(u_ref, gate_ref, fa_ref, fai_ref, gc_ref, gci_ref, kr_ref, ki_ref, bias_ref, o_ref,
                  uf_ref, s_ref, *, a_len, kp):
    up, sp = _pitch(FFT_NB), _pitch(2 * kp)
    _stage_rows(lambda r0: u_ref[0, pl.ds(r0, FFT_NB), :].astype(F32), uf_ref, a_len)
    _fft_stage_a(uf_ref, s_ref, fa_ref[...], a_len=a_len, kp=kp)

    def spectrum_product(g, carry):
        k1s = [g * FFT_GROUP + j for j in range(FFT_GROUP)]
        xs = [_fft_stage_c(s_ref, gc_ref, k1, kp) for k1 in k1s]
        outs = []
        for k1, (xr, xi) in zip(k1s, xs):
            kr, ki = kr_ref[k1], ki_ref[k1]
            y = jnp.concatenate([xr * kr - xi * ki, xr * ki + xi * kr], axis=0).astype(BF16)
            outs.append(_dot(gci_ref[k1], y))
        for k1, bc in zip(k1s, outs):
            _st(s_ref, k1, FFT_NB, bc[:FFT_NB], sp)
            _st(s_ref, kp + k1, FFT_NB, bc[FFT_NB:], sp)
        return carry
    lax.fori_loop(0, kp // FFT_GROUP, spectrum_product, 0)

    fai = fai_ref[...]

    def inverse_a(g, carry):
        offs = [pl.multiple_of((g * FFT_RGROUP + j) * sp, V7X_SUBLANES) for j in range(FFT_RGROUP)]
        slabs = [_ld(s_ref, off, 2 * kp).astype(BF16) for off in offs]
        for off, slab in zip(offs, slabs):
            _st(s_ref, off, a_len, _dot(fai, slab))
        return carry
    lax.fori_loop(0, FFT_NB // FFT_RGROUP, inverse_a, 0)

    bias = bias_ref[...]

    def epilogue(g, carry):
        for j in range(FFT_GROUP):
            a = g * FFT_GROUP + j
            r0 = pl.multiple_of(a * FFT_NB, FFT_NB)
            y = _ld(s_ref, a, FFT_NB, sp)
            uf = _ld(uf_ref, pl.multiple_of(a * up, V7X_SUBLANES), FFT_NB)
            gate = gate_ref[0, pl.ds(r0, FFT_NB), :].astype(F32)
            o_ref[0, pl.ds(r0, FFT_NB), :] = (gate * (y + uf * bias)).astype(o_ref.dtype)
        return carry
    lax.fori_loop(0, a_len // FFT_GROUP, epilogue, 0)


def _lconv(u_arr, u_col, gate_arr, gate_col, fa, fai, gc, gci, kr, ki, bias, order_idx, width, a_len, kp, tc=256):
    B, L, _ = u_arr.shape
    nh = width // tc
    const2 = lambda h, b: (0, 0)
    const3 = lambda h, b: (0, 0, 0)
    spec = lambda h, b: (0, 0, order_idx * nh + h)
    return pl.pallas_call(
        functools.partial(_lconv_kernel, a_len=a_len, kp=kp),
        grid=(nh, B),
        in_specs=[pl.BlockSpec((1, L, tc), lambda h, b: (b, 0, u_col * nh + h)),
                  pl.BlockSpec((1, L, tc), lambda h, b: (b, 0, gate_col * nh + h)),
                  pl.BlockSpec(fa.shape, const2), pl.BlockSpec(fai.shape, const2),
                  pl.BlockSpec(gc.shape, const3, pipeline_mode=pl.Buffered(1)),
                  pl.BlockSpec(gci.shape, const3, pipeline_mode=pl.Buffered(1)),
                  pl.BlockSpec((kp, FFT_NB, tc), spec, pipeline_mode=pl.Buffered(1)),
                  pl.BlockSpec((kp, FFT_NB, tc), spec, pipeline_mode=pl.Buffered(1)),
                  pl.BlockSpec((1, tc), lambda h, b: (0, h))],
        out_specs=pl.BlockSpec((1, L, tc), lambda h, b: (b, 0, h)),
        out_shape=jax.ShapeDtypeStruct((B, L, width), BF16),
        scratch_shapes=[pltpu.VMEM((tc // V7X_LANES, a_len * _pitch(FFT_NB), V7X_LANES), F32),
                        pltpu.VMEM((tc // V7X_LANES, FFT_NB * _pitch(2 * kp), V7X_LANES), F32)],
        compiler_params=_cparams(2, VMEM_LIMIT),
        name="lconv%d" % order_idx,
    )(u_arr, gate_arr, fa, fai, gc, gci, kr, ki, bias)


def _merge_kernel(x_ref, yhy_ref, oat_ref, g_ref, g1_ref, lng_ref, lnb_ref, l1g_ref, l1b_ref,
                  whyo_ref, wato_ref, wout_ref, o_ref, *, alpha):
    d = whyo_ref.shape[1]
    xln = _layer_norm(x_ref[0], lng_ref[...], lnb_ref[...])
    g = g_ref[0].astype(F32)
    m = (_sigmoid(g[:, :d]) * _dot(yhy_ref[0], whyo_ref[...])
         + _sigmoid(g[:, d:]) * _dot(oat_ref[0], wato_ref[...]))
    y = _dot(m.astype(BF16), wout_ref[...])
    o_ref[0] = _layer_norm(alpha * xln + g1_ref[0] * y, l1g_ref[...], l1b_ref[...])


def _merge(x, yhy, oat, g, g1, lng, lnb, l1g, l1b, whyo, wato, wout, alpha, tm=512):
    B, S, D = x.shape
    row = lambda b, i: (b, i, 0)
    mod = lambda b, i: (b, 0, 0)
    const = lambda b, i: (0, 0)
    vec = pl.BlockSpec((1, D), const)
    return pl.pallas_call(
        functools.partial(_merge_kernel, alpha=alpha),
        grid=(B, S // tm),
        in_specs=[pl.BlockSpec((1, tm, D), row),
                  pl.BlockSpec((1, tm, yhy.shape[2]), row), pl.BlockSpec((1, tm, oat.shape[2]), row),
                  pl.BlockSpec((1, tm, g.shape[2]), row),
                  pl.BlockSpec((1, 1, D), mod), vec, vec, vec, vec,
                  pl.BlockSpec(whyo.shape, const), pl.BlockSpec(wato.shape, const), pl.BlockSpec(wout.shape, const)],
        out_specs=pl.BlockSpec((1, tm, D), row),
        out_shape=jax.ShapeDtypeStruct((B, S, D), F32),
        compiler_params=_cparams(2, VMEM_LIMIT),
        name="merge",
    )(x, yhy, oat, g, g1, lng, lnb, l1g, l1b, whyo, wato, wout)


def _ffn_kernel(x_ref, xp_ref, xn_ref, sh_ref, sc_ref, g2_ref, wup_ref, cw_ref, cb_ref, wdn_ref,
                l2g_ref, l2b_ref, o_ref, *, alpha, d_ff, tc, halo):
    i = pl.program_id(1)
    n_i = pl.num_programs(1)
    x = x_ref[0]
    tm = x.shape[0]
    sc = 1.0 + sc_ref[0]
    sh = sh_ref[0]
    hp = jnp.where(i > 0, xp_ref[0] * sc + sh, 0.0)
    hn = jnp.where(i < n_i - 1, xn_ref[0] * sc + sh, 0.0)
    h = jnp.concatenate([hp, x * sc + sh, hn], axis=0).astype(BF16)
    rows = tm + 2 * halo

    def conv(u, lo, sz):
        w = cw_ref[:, lo:lo + sz]
        um = pltpu.roll(u, 1, 0)[halo:halo + tm]
        up = pltpu.roll(u, rows - 1, 0)[halo:halo + tm]
        return um * w[0:1] + u[halo:halo + tm] * w[1:2] + up * w[2:3] + cb_ref[:, lo:lo + sz]

    def up(lo, sz):
        return _dot(h, wup_ref[:, lo:lo + sz]), _dot(h, wup_ref[:, d_ff + lo:d_ff + lo + sz])

    chunks = [(lo, min(tc, d_ff - lo)) for lo in range(0, d_ff, tc)]
    cur = up(*chunks[0])
    acts = []
    for c, (lo, sz) in enumerate(chunks):
        nxt = up(*chunks[c + 1]) if c + 1 < len(chunks) else None
        a = conv(cur[0], lo, sz)
        g = conv(cur[1], d_ff + lo, sz)
        acts.append((g * _sigmoid(g) * a).astype(BF16))
        cur = nxt
    acc = _dot(jnp.concatenate(acts, axis=1), wdn_ref[...])
    o_ref[0] = _layer_norm(alpha * x + g2_ref[0] * acc, l2g_ref[...], l2b_ref[...])


def _ffn(x1, sh, sc, g2, wup, cw, cb, wdn, l2g, l2b, alpha, tm=512, tc=512):
    B, S, D = x1.shape
    d_ff = wdn.shape[0]
    halo = V7X_SUBLANES
    nb = tm // halo
    row = lambda b, i: (b, i, 0)
    prev = lambda b, i: (b, jnp.maximum(i * nb - 1, 0), 0)
    nxt = lambda b, i: (b, jnp.minimum((i + 1) * nb, S // halo - 1), 0)
    mod = lambda b, i: (b, 0, 0)
    const = lambda b, i: (0, 0)
    vec = pl.BlockSpec((1, D), const)
    return pl.pallas_call(
        functools.partial(_ffn_kernel, alpha=alpha, d_ff=d_ff, tc=tc, halo=halo),
        grid=(B, S // tm),
        in_specs=[pl.BlockSpec((1, tm, D), row), pl.BlockSpec((1, halo, D), prev), pl.BlockSpec((1, halo, D), nxt),
                  pl.BlockSpec((1, 1, D), mod), pl.BlockSpec((1, 1, D), mod), pl.BlockSpec((1, 1, D), mod),
                  pl.BlockSpec(wup.shape, const, pipeline_mode=pl.Buffered(1)),
                  pl.BlockSpec(cw.shape, const), pl.BlockSpec(cb.shape, const),
                  pl.BlockSpec(wdn.shape, const, pipeline_mode=pl.Buffered(1)),
                  vec, vec],
        out_specs=pl.BlockSpec((1, tm, D), row),
        out_shape=jax.ShapeDtypeStruct((B, S, D), F32),
        compiler_params=_cparams(2, VMEM_LIMIT),
        name="ffn",
    )(x1, x1, x1, sh, sc, g2, wup, cw, cb, wdn, l2g, l2b)


def _rope_tables(S, at_d):
    rope_axis = at_d // 2
    half = rope_axis // 2
    pos = np.arange(S)
    rowp = (pos // GRID_W).astype(np.float64)
    colp = (pos % GRID_W).astype(np.float64)
    inv = ROPE_BASE ** (-np.arange(0, rope_axis, 2, dtype=np.float64) / rope_axis)
    lane = np.arange(V7X_LANES)
    d = lane % at_d
    e = d % rope_axis
    f = e % half
    ang = np.where((d < rope_axis)[None, :], rowp[:, None], colp[:, None]) * inv[f][None, :]
    first = (e < half)[None, :]
    sin = np.sin(ang)
    as_f32 = lambda m: jnp.asarray(m, dtype=F32)
    return as_f32(np.cos(ang)), as_f32(np.where(first, -sin, 0.0)), as_f32(np.where(first, 0.0, sin))


def _filter_tables(L, emb, width):
    bands = (emb - 1) // 2
    t = np.linspace(0.0, 1.0, L)[:, None]
    w = (2.0 * np.pi / L) * np.arange(L, dtype=np.float64)[:, None]
    f = np.linspace(1e-4, bands - 1, bands)[None, :]
    z = np.concatenate([t, np.cos(f * w), -np.sin(f * w)], -1)
    min_decay = math.log(HY_DECAY_TARGET) / HY_SLOW_PCT
    max_decay = math.log(HY_DECAY_TARGET) / HY_FAST_PCT
    deltas = np.linspace(min_decay, max_decay, width)
    decay = np.exp(-t * np.abs(deltas)[None, :])
    epad = (-emb) % V7X_LANES
    z = np.pad(z, ((0, 0), (0, epad)))
    return jnp.asarray(z, dtype=F32), jnp.asarray(decay, dtype=F32)


def _fft_tables(L):
    a_len = L // FFT_NB
    assert a_len * FFT_NB == L and a_len % FFT_GROUP == 0
    n_fft, n1 = 2 * L, 2 * a_len
    kp = -(-(a_len + 1) // FFT_GROUP) * FFT_GROUP
    k1 = np.arange(kp)
    valid = (k1 <= a_len).astype(np.float64)
    ph = (np.outer(k1, np.arange(a_len)) % n1) * (2.0 * np.pi / n1)
    ca, sa = np.cos(ph) * valid[:, None], np.sin(ph) * valid[:, None]
    fa = np.concatenate([ca, -sa], axis=0)
    w = np.where((k1 == 0) | (k1 == a_len), 1.0, 2.0) / n_fft
    fai = np.concatenate([(ca * w[:, None]).T, (-sa * w[:, None]).T], axis=1)
    r = np.arange(FFT_NB)
    kk = k1[:, None, None] + n1 * r[None, :, None]
    th = ((kk * r[None, None, :]) % n_fft) * (2.0 * np.pi / n_fft)
    gr, gi = np.cos(th) * valid[:, None, None], -np.sin(th) * valid[:, None, None]
    gc = np.concatenate([np.concatenate([gr, -gi], 2), np.concatenate([gi, gr], 2)], 1)
    grt, git = gr.transpose(0, 2, 1), gi.transpose(0, 2, 1)
    gci = np.concatenate([np.concatenate([grt, git], 2), np.concatenate([-git, grt], 2)], 1)
    as_bf16 = lambda m: jnp.asarray(m, dtype=F32).astype(BF16)
    return a_len, kp, as_bf16(fa), as_bf16(fai), as_bf16(gc), as_bf16(gci)


def kernel(x, c, ctx, c_ctx, ln_in_g, ln_in_b, w_ada, b_ada, w_in, hy_conv_w, hy_conv_b, hy_f_w1, hy_f_b1, hy_f_w2, hy_f_b2, hy_f_w3, hy_f_b3, hy_f_freq, hy_f_wout, hy_bias, lam_q1, lam_k1, lam_q2, lam_k2, at_subln_g, w_hy_o, w_at_o, w_out, ln1_g, ln1_b, ffn_w_up, ffn_conv_w, ffn_conv_b, ffn_w_down, ln2_g, ln2_b):
    B, S, D = x.shape
    depth = w_ada.shape[0]
    assert depth == 1, "single-layer configuration only"
    l = 0
    order, width = hy_bias.shape[1], hy_bias.shape[2]
    at_d = lam_q1.shape[1]
    at_w = w_at_o.shape[1]
    hy_cols = (order + 1) * width
    emb = hy_f_w1.shape[1]
    alpha = (2.0 * depth) ** 0.25
    lam_init = 0.8 - 0.6 * math.exp(-0.3 * l)
    assert 2 * at_d == V7X_LANES and order == 2

    row2 = lambda a: a.reshape(1, -1)

    pad = (-(B + 1)) % V7X_SUBLANES
    cc = jnp.concatenate([c, c_ctx[None, :], jnp.zeros((pad, D), F32)], 0)
    mod = _ada(cc, w_ada[l], row2(b_ada[l]))
    sh1, sc1, g1, sh2, sc2, g2 = [mod[:B, i * D:(i + 1) * D].reshape(B, 1, D) for i in range(N_ADA)]
    sh1c, sc1c = mod[B:B + 1, 0:D], mod[B:B + 1, D:2 * D]

    lng, lnb = row2(ln_in_g), row2(ln_in_b)
    w_in_b = w_in[l].astype(BF16)
    cos, sin_a, sin_b = _rope_tables(S, at_d)
    u3, qt, k_l, vt_l, gates = _inproj(x, sh1, sc1, lng, lnb, w_in_b, hy_conv_w[l], row2(hy_conv_b[l]),
                                       cos, sin_a, sin_b, hy_cols, at_w, at_d)
    k_c, vt_c = _ctx_kv(ctx, sh1c, sc1c, lng, lnb, w_in_b[:, hy_cols + at_w:hy_cols + 3 * at_w], at_w)
    o_at = _attn(qt, k_l, vt_l, k_c, vt_c, row2(lam_q1[l]), row2(lam_k1[l]), row2(lam_q2[l]), row2(lam_k2[l]),
                 row2(at_subln_g[l]), lam_init, at_d)

    zemb, decay = _filter_tables(S, emb, width)
    w1p = jnp.pad(hy_f_w1[l], ((0, zemb.shape[1] - emb), (0, 0)))
    f_taps, b_taps = _filt(zemb, w1p, row2(hy_f_b1[l]), hy_f_w2[l], row2(hy_f_b2[l]), hy_f_w3[l],
                           row2(hy_f_b3[l]), hy_f_freq[l], hy_f_wout[l], decay, width, order)
    a_len, kp, fa, fai, gc, gci = _fft_tables(S)
    kr, ki = _fspec(f_taps, b_taps, fa, gc, a_len, kp)
    zz = _lconv(u3, 0, u3, 1, fa, fai, gc, gci, kr, ki, row2(hy_bias[l, 0]), 0, width, a_len, kp)
    y_hy = _lconv(zz, 0, u3, 2, fa, fai, gc, gci, kr, ki, row2(hy_bias[l, 1]), 1, width, a_len, kp)

    x1 = _merge(x, y_hy, o_at, gates, g1, lng, lnb, row2(ln1_g[l]), row2(ln1_b[l]),
                w_hy_o[l].astype(BF16), w_at_o[l].astype(BF16), w_out[l].astype(BF16), alpha)
    return _ffn(x1, sh2, sc2, g2, ffn_w_up[l].astype(BF16), ffn_conv_w[l], row2(ffn_conv_b[l]),
                ffn_w_down[l].astype(BF16), row2(ln2_g[l]), row2(ln2_b[l]), alpha)
```
